```python
import math
import jax, jax.numpy as jnp
from jax import lax
import numpy as np

D_MODEL = 1024
BATCH = 16
SEQ = 2048
DEPTH = 2

MEM_LEN = 256
ATTN_WIDTH = D_MODEL // 2
CONV_WIDTH_CH = D_MODEL - ATTN_WIDTH
DIFF_HEAD_DIM = 64
DIFF_HEADS = ATTN_WIDTH // (2 * DIFF_HEAD_DIM)
CONV_K = 3
CONV_GROUPS = 4
ROT_DIM = DIFF_HEAD_DIM // 4
ROPE_THETA = 500000.0
X_HEADS = 4
X_HEAD_DIM = D_MODEL // X_HEADS
D_FF = 4 * D_MODEL
Q_BLOCK = 128
NEG_INF = -1e30
NORM_EPS = 1e-6
SUBLN_EPS = 1e-5
IN_COLS = 3 * ATTN_WIDTH + 3 * CONV_WIDTH_CH

kernel_name = "hybrid_diffattn_shortconv_block"


def rms_norm(x, g, eps=NORM_EPS):
    xf = x.astype(jnp.float32)
    y = xf * lax.rsqrt(jnp.mean(xf * xf, axis=-1, keepdims=True) + eps)
    return (y * g.astype(jnp.float32)).astype(x.dtype)


def rotary_tables(positions):
    inv_freq = ROPE_THETA ** (-jnp.arange(0, ROT_DIM, 2, dtype=jnp.float32) / ROT_DIM)
    ang = positions.astype(jnp.float32)[..., None] * inv_freq
    return jnp.cos(ang), jnp.sin(ang)


def apply_partial_rotary(t, cos, sin):
    half = ROT_DIM // 2
    r1 = t[..., :half]
    r2 = t[..., half:ROT_DIM]
    rest = t[..., ROT_DIM:]
    c = cos[:, :, None, None, :].astype(t.dtype)
    s = sin[:, :, None, None, :].astype(t.dtype)
    return jnp.concatenate([r1 * c - r2 * s, r2 * c + r1 * s, rest], axis=-1)


def diff_attention(q, k, v, lam, subln_g, lambda_init):
    bsz, seq = q.shape[0], q.shape[1]
    scale = DIFF_HEAD_DIM ** -0.5
    qh = jnp.transpose(q, (0, 2, 3, 1, 4))
    kh = jnp.transpose(k, (0, 2, 3, 1, 4))
    vh = jnp.transpose(v, (0, 2, 1, 3))
    outs = []
    for start in range(0, seq, Q_BLOCK):
        end = start + Q_BLOCK
        qb = qh[:, :, :, start:end]
        kb = kh[:, :, :, :end]
        vb = vh[:, :, :end]
        s = jnp.einsum('bhcqd,bhckd->bhcqk', qb, kb).astype(jnp.float32) * scale
        mask = (start + jnp.arange(Q_BLOCK))[:, None] >= jnp.arange(end)[None, :]
        s = jnp.where(mask, s, NEG_INF)
        p = jax.nn.softmax(s, axis=-1)
        w = p[:, :, 0] - lam * p[:, :, 1]
        outs.append(jnp.einsum('bhqk,bhkd->bhqd', w.astype(vb.dtype), vb))
    o = jnp.concatenate(outs, axis=2)
    o = rms_norm(o, subln_g, eps=SUBLN_EPS) * (1.0 - lambda_init)
    return jnp.transpose(o, (0, 2, 1, 3)).reshape(bsz, seq, ATTN_WIDTH)


def short_gated_conv(b_gate, c_gate, h, conv_w):
    seq = h.shape[1]
    u = c_gate * h
    up = jnp.pad(u, ((0, 0), (CONV_K - 1, 0), (0, 0)))
    y = conv_w[0] * up[:, 0:seq]
    for j in range(1, CONV_K):
        y = y + conv_w[j] * up[:, j:j + seq]
    return b_gate * y


def cross_attention(xn, memn, w_q, w_kv, w_o):
    bsz, seq = xn.shape[0], xn.shape[1]
    q = (xn @ w_q).reshape(bsz, seq, X_HEADS, X_HEAD_DIM)
    kv = memn @ w_kv
    k = kv[..., :D_MODEL].reshape(bsz, MEM_LEN, X_HEADS, X_HEAD_DIM)
    v = kv[..., D_MODEL:].reshape(bsz, MEM_LEN, X_HEADS, X_HEAD_DIM)
    s = jnp.einsum('bshd,bmhd->bhsm', q, k).astype(jnp.float32) * (X_HEAD_DIM ** -0.5)
    p = jax.nn.softmax(s, axis=-1)
    o = jnp.einsum('bhsm,bmhd->bshd', p.astype(v.dtype), v).reshape(bsz, seq, D_MODEL)
    return o @ w_o


def setup_inputs(seed: int = 0) -> dict:
    key = jax.random.key(seed)
    ks = jax.random.split(key, 24)
    f32 = jnp.float32

    def w(k, shape, fan_in):
        return jax.random.normal(k, shape, f32) * (fan_in ** -0.5)

    def gain(k, shape):
        return 1.0 + 0.02 * jax.random.normal(k, shape, f32)

    x = jax.random.normal(ks[0], (BATCH, SEQ, D_MODEL), f32)
    mem = jax.random.normal(ks[1], (BATCH, MEM_LEN, D_MODEL), f32)
    offset = jax.random.randint(ks[2], (BATCH, 1), 0, 1024, dtype=jnp.int32)
    positions = (offset + jnp.arange(SEQ, dtype=jnp.int32)[None, :]).astype(jnp.int32)
    return {
        "x": x,
        "mem": mem,
        "positions": positions,
        "norm_mix_g": gain(ks[3], (DEPTH, D_MODEL)),
        "w_in": w(ks[4], (DEPTH, D_MODEL, IN_COLS), D_MODEL),
        "lam_q1": 0.1 * jax.random.normal(ks[5], (DEPTH, DIFF_HEAD_DIM), f32),
        "lam_k1": 0.1 * jax.random.normal(ks[6], (DEPTH, DIFF_HEAD_DIM), f32),
        "lam_q2": 0.1 * jax.random.normal(ks[7], (DEPTH, DIFF_HEAD_DIM), f32),
        "lam_k2": 0.1 * jax.random.normal(ks[8], (DEPTH, DIFF_HEAD_DIM), f32),
        "subln_g": gain(ks[9], (DEPTH, 2 * DIFF_HEAD_DIM)),
        "conv_w": w(ks[10], (DEPTH, CONV_K, CONV_WIDTH_CH), CONV_K),
        "w_mix_out": w(ks[11], (DEPTH, D_MODEL, D_MODEL), D_MODEL),
        "norm_x_g": gain(ks[12], (DEPTH, D_MODEL)),
        "mem_norm_g": gain(ks[13], (D_MODEL,)),
        "w_xq": w(ks[14], (DEPTH, D_MODEL, D_MODEL), D_MODEL),
        "w_xkv": w(ks[15], (DEPTH, D_MODEL, 2 * D_MODEL), D_MODEL),
        "w_xo": w(ks[16], (DEPTH, D_MODEL, D_MODEL), D_MODEL),
        "norm_ffn_g": gain(ks[17], (DEPTH, D_MODEL)),
        "w_ff1": w(ks[18], (DEPTH, D_MODEL, D_FF), D_MODEL),
        "w_ff2": w(ks[19], (DEPTH, D_FF, D_MODEL), D_FF),
        "final_g": gain(ks[20], (D_MODEL,)),
    }


def reference(x, mem, positions, norm_mix_g, w_in, lam_q1, lam_k1, lam_q2, lam_k2, subln_g,
              conv_w, w_mix_out, norm_x_g, mem_norm_g, w_xq, w_xkv, w_xo, norm_ffn_g,
              w_ff1, w_ff2, final_g):
    bsz, seq = x.shape[0], x.shape[1]
    cos, sin = rotary_tables(positions)
    memn = rms_norm(mem, mem_norm_g)
    a0, a1, a2, a3 = ATTN_WIDTH, 2 * ATTN_WIDTH, 3 * ATTN_WIDTH, 3 * ATTN_WIDTH + CONV_WIDTH_CH
    a4 = a3 + CONV_WIDTH_CH
    h = x
    for l in range(DEPTH):
        lambda_init = 0.8 - 0.6 * math.exp(-0.3 * l)
        xn = rms_norm(h, norm_mix_g[l])
        proj = xn @ w_in[l]
        q = proj[..., :a0].reshape(bsz, seq, DIFF_HEADS, 2, DIFF_HEAD_DIM)
        k = proj[..., a0:a1].reshape(bsz, seq, DIFF_HEADS, 2, DIFF_HEAD_DIM)
        v = proj[..., a1:a2].reshape(bsz, seq, DIFF_HEADS, 2 * DIFF_HEAD_DIM)
        b_gate = proj[..., a2:a3]
        c_gate = proj[..., a3:a4]
        hc = proj[..., a4:]
        q = apply_partial_rotary(q, cos, sin)
        k = apply_partial_rotary(k, cos, sin)
        lam = (jnp.exp(jnp.sum(lam_q1[l].astype(jnp.float32) * lam_k1[l].astype(jnp.float32)))
               - jnp.exp(jnp.sum(lam_q2[l].astype(jnp.float32) * lam_k2[l].astype(jnp.float32)))
               + lambda_init)
        attn_out = diff_attention(q, k, v, lam, subln_g[l], lambda_init)
        conv_out = short_gated_conv(b_gate, c_gate, hc, conv_w[l])
        mixed = jnp.concatenate([attn_out, conv_out.astype(attn_out.dtype)], axis=-1)
        h = h + mixed @ w_mix_out[l]
        h = h + cross_attention(rms_norm(h, norm_x_g[l]), memn, w_xq[l], w_xkv[l], w_xo[l])
        f = rms_norm(h, norm_ffn_g[l]) @ w_ff1[l]
        f = jnp.square(jax.nn.relu(f))
        h = h + f @ w_ff2[l]
    return rms_norm(h, final_g)
```

```python
import functools
import math

import jax
import jax.numpy as jnp
from jax import lax
from jax.experimental import pallas as pl
from jax.experimental.pallas import tpu as pltpu

D_MODEL = 1024
DEPTH = 2
MEM_LEN = 256
ATTN_WIDTH = D_MODEL // 2
CONV_WIDTH_CH = D_MODEL - ATTN_WIDTH
DIFF_HEAD_DIM = 64
DIFF_HEADS = ATTN_WIDTH // (2 * DIFF_HEAD_DIM)
HEAD_W = 2 * DIFF_HEAD_DIM
CONV_K = 3
ROT_DIM = DIFF_HEAD_DIM // 4
ROPE_THETA = 500000.0
X_HEADS = 4
X_HEAD_DIM = D_MODEL // X_HEADS
D_FF = 4 * D_MODEL
NEG_INF = -1e30
NORM_EPS = 1e-6
SUBLN_EPS = 1e-5

LANES = 128
SUBLANES = 8
VMEM_LIMIT = 56 * 1024 * 1024

ROW_TILE = 512
Q_TILE = 256
K_TILE = 256
FF_CHUNK = 1024

BF16 = jnp.bfloat16
F32 = jnp.float32


def _rms(x, g, eps):
    return x * lax.rsqrt(jnp.mean(x * x, axis=-1, keepdims=True) + eps) * g


def _dot(a, b):
    return jnp.dot(a, b, preferred_element_type=F32)


def _dot_nt(a, b):
    return lax.dot_general(a, b, (((1,), (1,)), ((), ())), preferred_element_type=F32)


def _in_proj_kernel(pos_ref, rot_ref, x_ref, g_ref, w_ref, cw_ref,
                    q_ref, k_ref, v_ref, c_ref, carry_ref):
    tm = x_ref.shape[1]
    aw = ATTN_WIDTH
    xn = _rms(x_ref[0], g_ref[...], NORM_EPS).astype(BF16)

    ang = pos_ref[0].astype(F32) * rot_ref[0:1, :]
    cos = jnp.cos(ang)
    sin = jnp.sin(ang)
    sin_lo = sin * rot_ref[1:2, :]
    sin_hi = sin * rot_ref[2:3, :]
    half = ROT_DIM // 2

    def rotate(t):
        up = pltpu.roll(t, LANES - half, axis=1)
        dn = pltpu.roll(t, half, axis=1)
        return t * cos + up * sin_lo + dn * sin_hi

    scale = DIFF_HEAD_DIM ** -0.5
    for h in range(DIFF_HEADS):
        lo, hi = h * HEAD_W, (h + 1) * HEAD_W
        q_ref[0, :, lo:hi] = (rotate(_dot(xn, w_ref[:, lo:hi])) * scale).astype(BF16)
        k_ref[0, :, lo:hi] = rotate(_dot(xn, w_ref[:, aw + lo:aw + hi])).astype(BF16)
    v_ref[0] = _dot(xn, w_ref[:, 2 * aw:3 * aw]).astype(BF16)

    cwid = CONV_WIDTH_CH
    b_gate = _dot(xn, w_ref[:, 3 * aw:3 * aw + cwid])
    c_gate = _dot(xn, w_ref[:, 3 * aw + cwid:3 * aw + 2 * cwid])
    hc = _dot(xn, w_ref[:, 3 * aw + 2 * cwid:3 * aw + 3 * cwid])
    u = c_gate * hc

    @pl.when(pl.program_id(1) == 0)
    def _():
        carry_ref[...] = jnp.zeros_like(carry_ref)

    prev = carry_ref[...]
    row = lax.broadcasted_iota(jnp.int32, u.shape, 0)
    last = prev[SUBLANES - 1:SUBLANES, :]
    last2 = prev[SUBLANES - 2:SUBLANES - 1, :]
    u1 = jnp.where(row == 0, last, pltpu.roll(u, 1, axis=0))
    u2 = jnp.where(row == 0, last2, jnp.where(row == 1, last, pltpu.roll(u, 2, axis=0)))
    y = cw_ref[0:1, :] * u2 + cw_ref[1:2, :] * u1 + cw_ref[2:3, :] * u
    c_ref[0] = (b_gate * y).astype(BF16)
    carry_ref[...] = u[tm - SUBLANES:tm, :]


def _in_proj(h, pos3, rot_tab, g, w_in, conv_w):
    b, s, d = h.shape
    tm = ROW_TILE
    n_cols = w_in.shape[1]
    act = jax.ShapeDtypeStruct((b, s, ATTN_WIDTH), BF16)
    act_spec = pl.BlockSpec((1, tm, ATTN_WIDTH), lambda i, j: (i, j, 0))
    return pl.pallas_call(
        _in_proj_kernel,
        grid=(b, s // tm),
        in_specs=[
            pl.BlockSpec((1, tm, 1), lambda i, j: (i, j, 0)),
            pl.BlockSpec((SUBLANES, LANES), lambda i, j: (0, 0)),
            pl.BlockSpec((1, tm, d), lambda i, j: (i, j, 0)),
            pl.BlockSpec((1, d), lambda i, j: (0, 0)),
            pl.BlockSpec((d, n_cols), lambda i, j: (0, 0)),
            pl.BlockSpec((CONV_K, CONV_WIDTH_CH), lambda i, j: (0, 0)),
        ],
        out_specs=[act_spec, act_spec, act_spec, act_spec],
        out_shape=[act, act, act, act],
        scratch_shapes=[pltpu.VMEM((SUBLANES, CONV_WIDTH_CH), F32)],
        compiler_params=pltpu.CompilerParams(
            dimension_semantics=("arbitrary", "arbitrary"), vmem_limit_bytes=VMEM_LIMIT),
        name="in_proj",
    )(pos3, rot_tab, h, g, w_in, conv_w)


def _diff_attn_kernel(lam_ref, q_ref, k_ref, v_ref, g_ref, o_ref, m_ref, l_ref, acc_ref,
                      *, out_scale, lambda_init):
    tq = q_ref.shape[1]
    tk = K_TILE
    qi = pl.program_id(1)
    lam = (jnp.exp(jnp.sum(lam_ref[0:1, :] * lam_ref[1:2, :], axis=-1, keepdims=True))
           - jnp.exp(jnp.sum(lam_ref[2:3, :] * lam_ref[3:4, :], axis=-1, keepdims=True))
           + lambda_init)
    lane = lax.broadcasted_iota(jnp.int32, (tq, HEAD_W), 1)
    first = lane < DIFF_HEAD_DIM
    rows = qi * tq + lax.broadcasted_iota(jnp.int32, (tq, tk), 0)
    cols = lax.broadcasted_iota(jnp.int32, (tq, tk), 1)

    for h in range(DIFF_HEADS):
        lo, hi = h * HEAD_W, (h + 1) * HEAD_W
        qh = q_ref[0, :, lo:hi]
        zero = jnp.zeros_like(qh)
        qc = (jnp.where(first, qh, zero), jnp.where(first, zero, qh))
        m_ref[...] = jnp.full_like(m_ref, NEG_INF)
        l_ref[...] = jnp.zeros_like(l_ref)
        acc_ref[...] = jnp.zeros_like(acc_ref)

        def step(j, masked):
            start = pl.multiple_of(j * tk, tk)
            kj = k_ref[0, pl.ds(start, tk), lo:hi]
            vj = v_ref[0, pl.ds(start, tk), lo:hi]
            for c in range(2):
                s = _dot_nt(qc[c], kj)
                if masked:
                    s = jnp.where(rows >= start + cols, s, NEG_INF)
                m_old = m_ref[c]
                m_new = jnp.maximum(m_old, jnp.max(s, axis=-1, keepdims=True))
                alpha = jnp.exp(m_old - m_new)
                p = jnp.exp(s - m_new)
                l_ref[c] = alpha * l_ref[c] + jnp.sum(p, axis=-1, keepdims=True)
                acc_ref[c] = alpha * acc_ref[c] + _dot(p.astype(BF16), vj)
                m_ref[c] = m_new

        n_full = (qi * tq) // tk
        n_all = ((qi + 1) * tq + tk - 1) // tk

        def full_body(j, carry):
            step(j, False)
            return carry

        def diag_body(j, carry):
            step(j, True)
            return carry

        lax.fori_loop(0, n_full, full_body, 0)
        lax.fori_loop(n_full, n_all, diag_body, 0)

        o = acc_ref[0] / l_ref[0] - lam * (acc_ref[1] / l_ref[1])
        o = _rms(o, g_ref[...], SUBLN_EPS) * out_scale
        o_ref[0, :, lo:hi] = o.astype(o_ref.dtype)


def _diff_attn(lam_rows, q, k, v, subln_g, lambda_init):
    b, s, w = q.shape
    tq = Q_TILE
    kern = functools.partial(_diff_attn_kernel, out_scale=1.0 - lambda_init, lambda_init=lambda_init)
    return pl.pallas_call(
        kern,
        grid=(b, s // tq),
        in_specs=[
            pl.BlockSpec((4, DIFF_HEAD_DIM), lambda i, j: (0, 0)),
            pl.BlockSpec((1, tq, w), lambda i, j: (i, j, 0)),
            pl.BlockSpec((1, s, w), lambda i, j: (i, 0, 0)),
            pl.BlockSpec((1, s, w), lambda i, j: (i, 0, 0)),
            pl.BlockSpec((1, HEAD_W), lambda i, j: (0, 0)),
        ],
        out_specs=pl.BlockSpec((1, tq, w), lambda i, j: (i, j, 0)),
        out_shape=jax.ShapeDtypeStruct((b, s, w), BF16),
        scratch_shapes=[
            pltpu.VMEM((2, tq, 1), F32),
            pltpu.VMEM((2, tq, 1), F32),
            pltpu.VMEM((2, tq, HEAD_W), F32),
        ],
        compiler_params=pltpu.CompilerParams(
            dimension_semantics=("arbitrary", "arbitrary"), vmem_limit_bytes=VMEM_LIMIT),
        name="diff_attn",
    )(lam_rows, q, k, v, subln_g)


def _mix_xattn_kernel(h_ref, a_ref, c_ref, wmo_ref, g_ref, wq_ref, kv_ref, wo_ref, o_ref):
    aw = ATTN_WIDTH
    h1 = h_ref[0] + _dot(a_ref[0], wmo_ref[0:aw, :]) + _dot(c_ref[0], wmo_ref[aw:, :])
    xn = _rms(h1, g_ref[...], NORM_EPS).astype(BF16)
    scale = X_HEAD_DIM ** -0.5
    q = (_dot(xn, wq_ref[...]) * scale).astype(BF16)
    heads = []
    for hh in range(X_HEADS):
        lo, hi = hh * X_HEAD_DIM, (hh + 1) * X_HEAD_DIM
        s = _dot_nt(q[:, lo:hi], kv_ref[0, :, lo:hi])
        p = jnp.exp(s - jnp.max(s, axis=-1, keepdims=True))
        l = jnp.sum(p, axis=-1, keepdims=True)
        o = _dot(p.astype(BF16), kv_ref[0, :, D_MODEL + lo:D_MODEL + hi]) / l
        heads.append(o.astype(BF16))
    o_all = jnp.concatenate(heads, axis=-1)
    o_ref[0] = h1 + _dot(o_all, wo_ref[...])


def _mix_xattn(h, attn, conv, w_mo, g, w_xq, kv, w_xo):
    b, s, d = h.shape
    tm = ROW_TILE
    full = lambda shape: pl.BlockSpec(shape, lambda i, j: (0,) * len(shape))
    return pl.pallas_call(
        _mix_xattn_kernel,
        grid=(b, s // tm),
        in_specs=[
            pl.BlockSpec((1, tm, d), lambda i, j: (i, j, 0)),
            pl.BlockSpec((1, tm, ATTN_WIDTH), lambda i, j: (i, j, 0)),
            pl.BlockSpec((1, tm, CONV_WIDTH_CH), lambda i, j: (i, j, 0)),
            full((d, d)),
            full((1, d)),
            full((d, d)),
            pl.BlockSpec((1, MEM_LEN, 2 * d), lambda i, j: (i, 0, 0)),
            full((d, d)),
        ],
        out_specs=pl.BlockSpec((1, tm, d), lambda i, j: (i, j, 0)),
        out_shape=jax.ShapeDtypeStruct((b, s, d), F32),
        compiler_params=pltpu.CompilerParams(
            dimension_semantics=("arbitrary", "arbitrary"), vmem_limit_bytes=VMEM_LIMIT),
        name="mix_xattn",
    )(h, attn, conv, w_mo, g, w_xq, kv, w_xo)


def _ffn_kernel(h_ref, g_ref, w1_ref, w2_ref, fg_ref, o_ref, acc_ref, *, final_norm):
    h = h_ref[...]
    xn = _rms(h, g_ref[...], NORM_EPS).astype(BF16)
    for c in range(D_FF // FF_CHUNK):
        lo, hi = c * FF_CHUNK, (c + 1) * FF_CHUNK
        f = jnp.square(jnp.maximum(_dot(xn, w1_ref[:, lo:hi]), 0.0)).astype(BF16)
        part = _dot(f, w2_ref[lo:hi, :])
        if c == 0:
            acc_ref[...] = part
        else:
            acc_ref[...] += part
    out = h + acc_ref[...]
    if final_norm:
        out = _rms(out, fg_ref[...], NORM_EPS)
    o_ref[...] = out


def _ffn(h2d, g, w1, w2, final_g, final_norm):
    n, d = h2d.shape
    tm = ROW_TILE
    full = lambda shape: pl.BlockSpec(shape, lambda i: (0,) * len(shape))
    return pl.pallas_call(
        functools.partial(_ffn_kernel, final_norm=final_norm),
        grid=(n // tm,),
        in_specs=[
            pl.BlockSpec((tm, d), lambda i: (i, 0)),
            full((1, d)),
            full((d, D_FF)),
            full((D_FF, d)),
            full((1, d)),
        ],
        out_specs=pl.BlockSpec((tm, d), lambda i: (i, 0)),
        out_shape=jax.ShapeDtypeStruct((n, d), F32),
        scratch_shapes=[pltpu.VMEM((tm, d), F32)],
        compiler_params=pltpu.CompilerParams(
            dimension_semantics=("arbitrary",), vmem_limit_bytes=VMEM_LIMIT),
        name="ffn",
    )(h2d, g, w1, w2, final_g)


def _mem_kv_kernel(m_ref, g_ref, w_ref, o_ref):
    mn = _rms(m_ref[...], g_ref[...], NORM_EPS).astype(BF16)
    o_ref[0] = _dot(mn, w_ref[0]).astype(BF16)


def _mem_kv(mem2d, g, w_xkv):
    n, d = mem2d.shape
    depth, _, n_cols = w_xkv.shape
    tm = ROW_TILE
    return pl.pallas_call(
        _mem_kv_kernel,
        grid=(depth, n // tm),
        in_specs=[
            pl.BlockSpec((tm, d), lambda l, i: (i, 0)),
            pl.BlockSpec((1, d), lambda l, i: (0, 0)),
            pl.BlockSpec((1, d, n_cols), lambda l, i: (l, 0, 0)),
        ],
        out_specs=pl.BlockSpec((1, tm, n_cols), lambda l, i: (l, i, 0)),
        out_shape=jax.ShapeDtypeStruct((depth, n, n_cols), BF16),
        compiler_params=pltpu.CompilerParams(
            dimension_semantics=("arbitrary", "arbitrary"), vmem_limit_bytes=VMEM_LIMIT),
        name="mem_kv",
    )(mem2d, g, w_xkv)


def _rotary_lane_table():
    half = ROT_DIM // 2
    inv_freq = ROPE_THETA ** (-jnp.arange(0, ROT_DIM, 2, dtype=F32) / ROT_DIM)
    d = jnp.arange(LANES) % DIFF_HEAD_DIM
    freq = jnp.where(d < ROT_DIM, inv_freq[d % half], 0.0)
    lo = jnp.where(d < half, -1.0, 0.0)
    hi = jnp.where((d >= half) & (d < ROT_DIM), 1.0, 0.0)
    tab = jnp.zeros((SUBLANES, LANES), F32)
    return tab.at[0].set(freq).at[1].set(lo).at[2].set(hi)


def kernel(x, mem, positions, norm_mix_g, w_in, lam_q1, lam_k1, lam_q2, lam_k2, subln_g, conv_w,
           w_mix_out, norm_x_g, mem_norm_g, w_xq, w_xkv, w_xo, norm_ffn_g, w_ff1, w_ff2, final_g):
    b, s, d = x.shape
    pos3 = positions.reshape(b, s, 1)
    rot_tab = _rotary_lane_table()
    kv = _mem_kv(mem.reshape(b * MEM_LEN, d), mem_norm_g.reshape(1, d), w_xkv.astype(BF16))
    kv = kv.reshape(DEPTH, b, MEM_LEN, 2 * d)
    h = x
    for l in range(DEPTH):
        lambda_init = 0.8 - 0.6 * math.exp(-0.3 * l)
        q, k, v, conv = _in_proj(h, pos3, rot_tab, norm_mix_g[l].reshape(1, d),
                                 w_in[l].astype(BF16), conv_w[l])
        lam_rows = jnp.stack([lam_q1[l], lam_k1[l], lam_q2[l], lam_k2[l]]).astype(F32)
        attn = _diff_attn(lam_rows, q, k, v, subln_g[l].reshape(1, HEAD_W), lambda_init)
        h = _mix_xattn(h, attn, conv, w_mix_out[l].astype(BF16), norm_x_g[l].reshape(1, d),
                       w_xq[l].astype(BF16), kv[l], w_xo[l].astype(BF16))
        h = _ffn(h.reshape(b * s, d), norm_ffn_g[l].reshape(1, d), w_ff1[l].astype(BF16),
                 w_ff2[l].astype(BF16), final_g.reshape(1, d), l == DEPTH - 1).reshape(b, s, d)
    return h
```

```python
import functools
import math

import jax
import jax.numpy as jnp
from jax import lax
from jax.experimental import pallas as pl
from jax.experimental.pallas import tpu as pltpu

D_MODEL = 1024
DEPTH = 2
MEM_LEN = 256
ATTN_WIDTH = D_MODEL // 2
CONV_WIDTH_CH = D_MODEL - ATTN_WIDTH
DIFF_HEAD_DIM = 64
DIFF_HEADS = ATTN_WIDTH // (2 * DIFF_HEAD_DIM)
HEAD_W = 2 * DIFF_HEAD_DIM
CONV_K = 3
ROT_DIM = DIFF_HEAD_DIM // 4
ROPE_THETA = 500000.0
X_HEADS = 4
X_HEAD_DIM = D_MODEL // X_HEADS
D_FF = 4 * D_MODEL
NEG_INF = -1e30
NORM_EPS = 1e-6
SUBLN_EPS = 1e-5

LANES = 128
SUBLANES = 8
VMEM_LIMIT = 56 * 1024 * 1024

ROW_TILE = 512
Q_TILE = 512
K_TILE = ROW_TILE
FF_CHUNK = 1024

BF16 = jnp.bfloat16
F32 = jnp.float32


def _rms(x, g, eps):
    return x * lax.rsqrt(jnp.mean(x * x, axis=-1, keepdims=True) + eps) * g


def _dot(a, b):
    return jnp.dot(a, b, preferred_element_type=F32)


def _dot_nt(a, b):
    return lax.dot_general(a, b, (((1,), (1,)), ((), ())), preferred_element_type=F32)


def _in_proj_kernel(pos_ref, rot_ref, x_ref, g_ref, w_ref, wvt_ref, cw_ref,
                    q_ref, k_ref, vt_ref, c_ref, carry_ref):
    tm = x_ref.shape[1]
    aw = ATTN_WIDTH
    xn = _rms(x_ref[0], g_ref[...], NORM_EPS).astype(BF16)

    ang = pos_ref[0].astype(F32) * rot_ref[0:1, :]
    cos = jnp.cos(ang)
    sin = jnp.sin(ang)
    sin_lo = sin * rot_ref[1:2, :]
    sin_hi = sin * rot_ref[2:3, :]
    half = ROT_DIM // 2

    def rotate(t):
        up = pltpu.roll(t, LANES - half, axis=1)
        dn = pltpu.roll(t, half, axis=1)
        return t * cos + up * sin_lo + dn * sin_hi

    scale = DIFF_HEAD_DIM ** -0.5
    for h in range(DIFF_HEADS):
        lo, hi = h * HEAD_W, (h + 1) * HEAD_W
        q_ref[0, :, lo:hi] = (rotate(_dot(xn, w_ref[:, lo:hi])) * scale).astype(BF16)
        k_ref[0, :, lo:hi] = rotate(_dot(xn, w_ref[:, aw + lo:aw + hi])).astype(BF16)
    vt_ref[0, 0] = _dot_nt(wvt_ref[...], xn).astype(BF16)

    cwid = CONV_WIDTH_CH
    b_gate = _dot(xn, w_ref[:, 3 * aw:3 * aw + cwid])
    c_gate = _dot(xn, w_ref[:, 3 * aw + cwid:3 * aw + 2 * cwid])
    hc = _dot(xn, w_ref[:, 3 * aw + 2 * cwid:3 * aw + 3 * cwid])
    u = c_gate * hc

    @pl.when(pl.program_id(1) == 0)
    def _():
        carry_ref[...] = jnp.zeros_like(carry_ref)

    prev = carry_ref[...]
    row = lax.broadcasted_iota(jnp.int32, u.shape, 0)
    last = prev[SUBLANES - 1:SUBLANES, :]
    last2 = prev[SUBLANES - 2:SUBLANES - 1, :]
    u1 = jnp.where(row == 0, last, pltpu.roll(u, 1, axis=0))
    u2 = jnp.where(row == 0, last2, jnp.where(row == 1, last, pltpu.roll(u, 2, axis=0)))
    y = cw_ref[0:1, :] * u2 + cw_ref[1:2, :] * u1 + cw_ref[2:3, :] * u
    c_ref[0] = (b_gate * y).astype(BF16)
    carry_ref[...] = u[tm - SUBLANES:tm, :]


def _in_proj(h, pos3, rot_tab, g, w_in, w_vt, conv_w):
    b, s, d = h.shape
    tm = ROW_TILE
    n_cols = w_in.shape[1]
    act = jax.ShapeDtypeStruct((b, s, ATTN_WIDTH), BF16)
    act_spec = pl.BlockSpec((1, tm, ATTN_WIDTH), lambda i, j: (i, j, 0))
    vt = jax.ShapeDtypeStruct((b, s // tm, ATTN_WIDTH, tm), BF16)
    vt_spec = pl.BlockSpec((1, 1, ATTN_WIDTH, tm), lambda i, j: (i, j, 0, 0))
    return pl.pallas_call(
        _in_proj_kernel,
        grid=(b, s // tm),
        in_specs=[
            pl.BlockSpec((1, tm, 1), lambda i, j: (i, j, 0)),
            pl.BlockSpec((SUBLANES, LANES), lambda i, j: (0, 0)),
            pl.BlockSpec((1, tm, d), lambda i, j: (i, j, 0)),
            pl.BlockSpec((1, d), lambda i, j: (0, 0)),
            pl.BlockSpec((d, n_cols), lambda i, j: (0, 0)),
            pl.BlockSpec((ATTN_WIDTH, d), lambda i, j: (0, 0)),
            pl.BlockSpec((CONV_K, CONV_WIDTH_CH), lambda i, j: (0, 0)),
        ],
        out_specs=[act_spec, act_spec, vt_spec, act_spec],
        out_shape=[act, act, vt, act],
        scratch_shapes=[pltpu.VMEM((SUBLANES, CONV_WIDTH_CH), F32)],
        compiler_params=pltpu.CompilerParams(
            dimension_semantics=("arbitrary", "arbitrary"), vmem_limit_bytes=VMEM_LIMIT),
        name="in_proj",
    )(pos3, rot_tab, h, g, w_in, w_vt, conv_w)


def _diff_attn_kernel(lam_ref, q_ref, k_ref, vt_ref, g_ref, o_ref, m_ref, l_ref, acc_ref,
                      *, out_scale, lambda_init):
    tq = q_ref.shape[1]
    tk = vt_ref.shape[3]
    qi = pl.program_id(2)
    lam = (jnp.exp(jnp.sum(lam_ref[0:1, :] * lam_ref[1:2, :], axis=-1, keepdims=True))
           - jnp.exp(jnp.sum(lam_ref[2:3, :] * lam_ref[3:4, :], axis=-1, keepdims=True))
           + lambda_init)
    qh = q_ref[0]
    lane = lax.broadcasted_iota(jnp.int32, qh.shape, 1)
    zero = jnp.zeros_like(qh)
    qc = (jnp.where(lane < DIFF_HEAD_DIM, qh, zero), jnp.where(lane < DIFF_HEAD_DIM, zero, qh))
    m_ref[...] = jnp.full_like(m_ref, NEG_INF)
    l_ref[...] = jnp.zeros_like(l_ref)
    acc_ref[...] = jnp.zeros_like(acc_ref)
    q_pos = qi * tq + lax.broadcasted_iota(jnp.int32, (tk, tq), 1)
    k_off = lax.broadcasted_iota(jnp.int32, (tk, tq), 0)

    def step(j, masked):
        start = pl.multiple_of(j * tk, tk)
        kj = k_ref[0, pl.ds(start, tk), :]
        vtj = vt_ref[0, j]
        for c in range(2):
            s = _dot_nt(kj, qc[c])
            if masked:
                s = jnp.where(q_pos >= start + k_off, s, NEG_INF)
            m_old = m_ref[c]
            m_new = jnp.maximum(m_old, jnp.max(s, axis=0, keepdims=True))
            alpha = jnp.exp(m_old - m_new)
            p = jnp.exp(s - m_new)
            l_ref[c] = alpha * l_ref[c] + jnp.sum(p, axis=0, keepdims=True)
            acc_ref[c] = alpha * acc_ref[c] + _dot(vtj, p.astype(BF16))
            m_ref[c] = m_new

    n_full = (qi * tq) // tk
    n_all = ((qi + 1) * tq + tk - 1) // tk

    def full_body(j, carry):
        step(j, False)
        return carry

    def diag_body(j, carry):
        step(j, True)
        return carry

    lax.fori_loop(0, n_full, full_body, 0)
    lax.fori_loop(n_full, n_all, diag_body, 0)

    ot = acc_ref[0] / l_ref[0] - lam * (acc_ref[1] / l_ref[1])
    ms = jnp.mean(ot * ot, axis=0, keepdims=True)
    ot = ot * lax.rsqrt(ms + SUBLN_EPS) * g_ref[...] * out_scale
    o_ref[0] = ot.T.astype(o_ref.dtype)


def _diff_attn(lam_rows, q, k, vt, subln_g_col, lambda_init):
    b, s, w = q.shape
    tq = Q_TILE
    nk, tk = vt.shape[1], vt.shape[3]
    kern = functools.partial(_diff_attn_kernel, out_scale=1.0 - lambda_init, lambda_init=lambda_init)
    return pl.pallas_call(
        kern,
        grid=(b, DIFF_HEADS, s // tq),
        in_specs=[
            pl.BlockSpec((4, DIFF_HEAD_DIM), lambda i, h, j: (0, 0)),
            pl.BlockSpec((1, tq, HEAD_W), lambda i, h, j: (i, j, h)),
            pl.BlockSpec((1, s, HEAD_W), lambda i, h, j: (i, 0, h)),
            pl.BlockSpec((1, nk, HEAD_W, tk), lambda i, h, j: (i, 0, h, 0)),
            pl.BlockSpec((HEAD_W, 1), lambda i, h, j: (0, 0)),
        ],
        out_specs=pl.BlockSpec((1, tq, HEAD_W), lambda i, h, j: (i, j, h)),
        out_shape=jax.ShapeDtypeStruct((b, s, w), BF16),
        scratch_shapes=[
            pltpu.VMEM((2, 1, tq), F32),
            pltpu.VMEM((2, 1, tq), F32),
            pltpu.VMEM((2, HEAD_W, tq), F32),
        ],
        compiler_params=pltpu.CompilerParams(
            dimension_semantics=("arbitrary", "arbitrary", "arbitrary"), vmem_limit_bytes=VMEM_LIMIT),
        name="diff_attn",
    )(lam_rows, q, k, vt, subln_g_col)


def _mix_xattn_kernel(h_ref, a_ref, c_ref, wmo_ref, g_ref, wq_ref, kv_ref, wo_ref, o_ref):
    aw = ATTN_WIDTH
    h1 = h_ref[0] + _dot(a_ref[0], wmo_ref[0:aw, :]) + _dot(c_ref[0], wmo_ref[aw:, :])
    xn = _rms(h1, g_ref[...], NORM_EPS).astype(BF16)
    scale = X_HEAD_DIM ** -0.5
    q = (_dot(xn, wq_ref[...]) * scale).astype(BF16)
    heads = []
    for hh in range(X_HEADS):
        lo, hi = hh * X_HEAD_DIM, (hh + 1) * X_HEAD_DIM
        s = _dot_nt(q[:, lo:hi], kv_ref[0, :, lo:hi])
        p = jnp.exp(s - jnp.max(s, axis=-1, keepdims=True))
        l = jnp.sum(p, axis=-1, keepdims=True)
        o = _dot(p.astype(BF16), kv_ref[0, :, D_MODEL + lo:D_MODEL + hi]) / l
        heads.append(o.astype(BF16))
    o_all = jnp.concatenate(heads, axis=-1)
    o_ref[0] = h1 + _dot(o_all, wo_ref[...])


def _mix_xattn(h, attn, conv, w_mo, g, w_xq, kv, w_xo):
    b, s, d = h.shape
    tm = ROW_TILE
    full = lambda shape: pl.BlockSpec(shape, lambda i, j: (0,) * len(shape))
    return pl.pallas_call(
        _mix_xattn_kernel,
        grid=(b, s // tm),
        in_specs=[
            pl.BlockSpec((1, tm, d), lambda i, j: (i, j, 0)),
            pl.BlockSpec((1, tm, ATTN_WIDTH), lambda i, j: (i, j, 0)),
            pl.BlockSpec((1, tm, CONV_WIDTH_CH), lambda i, j: (i, j, 0)),
            full((d, d)),
            full((1, d)),
            full((d, d)),
            pl.BlockSpec((1, MEM_LEN, 2 * d), lambda i, j: (i, 0, 0)),
            full((d, d)),
        ],
        out_specs=pl.BlockSpec((1, tm, d), lambda i, j: (i, j, 0)),
        out_shape=jax.ShapeDtypeStruct((b, s, d), F32),
        compiler_params=pltpu.CompilerParams(
            dimension_semantics=("arbitrary", "arbitrary"), vmem_limit_bytes=VMEM_LIMIT),
        name="mix_xattn",
    )(h, attn, conv, w_mo, g, w_xq, kv, w_xo)


def _ffn_kernel(h_ref, g_ref, w1_ref, w2_ref, fg_ref, o_ref, acc_ref, *, final_norm):
    h = h_ref[...]
    xn = _rms(h, g_ref[...], NORM_EPS).astype(BF16)
    for c in range(D_FF // FF_CHUNK):
        lo, hi = c * FF_CHUNK, (c + 1) * FF_CHUNK
        f = jnp.square(jnp.maximum(_dot(xn, w1_ref[:, lo:hi]), 0.0)).astype(BF16)
        part = _dot(f, w2_ref[lo:hi, :])
        if c == 0:
            acc_ref[...] = part
        else:
            acc_ref[...] += part
    out = h + acc_ref[...]
    if final_norm:
        out = _rms(out, fg_ref[...], NORM_EPS)
    o_ref[...] = out


def _ffn(h2d, g, w1, w2, final_g, final_norm):
    n, d = h2d.shape
    tm = ROW_TILE
    full = lambda shape: pl.BlockSpec(shape, lambda i: (0,) * len(shape))
    return pl.pallas_call(
        functools.partial(_ffn_kernel, final_norm=final_norm),
        grid=(n // tm,),
        in_specs=[
            pl.BlockSpec((tm, d), lambda i: (i, 0)),
            full((1, d)),
            full((d, D_FF)),
            full((D_FF, d)),
            full((1, d)),
        ],
        out_specs=pl.BlockSpec((tm, d), lambda i: (i, 0)),
        out_shape=jax.ShapeDtypeStruct((n, d), F32),
        scratch_shapes=[pltpu.VMEM((tm, d), F32)],
        compiler_params=pltpu.CompilerParams(
            dimension_semantics=("arbitrary",), vmem_limit_bytes=VMEM_LIMIT),
        name="ffn",
    )(h2d, g, w1, w2, final_g)


def _mem_kv_kernel(m_ref, g_ref, w_ref, o_ref):
    mn = _rms(m_ref[...], g_ref[...], NORM_EPS).astype(BF16)
    o_ref[0] = _dot(mn, w_ref[0]).astype(BF16)


def _mem_kv(mem2d, g, w_xkv):
    n, d = mem2d.shape
    depth, _, n_cols = w_xkv.shape
    tm = ROW_TILE
    return pl.pallas_call(
        _mem_kv_kernel,
        grid=(depth, n // tm),
        in_specs=[
            pl.BlockSpec((tm, d), lambda l, i: (i, 0)),
            pl.BlockSpec((1, d), lambda l, i: (0, 0)),
            pl.BlockSpec((1, d, n_cols), lambda l, i: (l, 0, 0)),
        ],
        out_specs=pl.BlockSpec((1, tm, n_cols), lambda l, i: (l, i, 0)),
        out_shape=jax.ShapeDtypeStruct((depth, n, n_cols), BF16),
        compiler_params=pltpu.CompilerParams(
            dimension_semantics=("arbitrary", "arbitrary"), vmem_limit_bytes=VMEM_LIMIT),
        name="mem_kv",
    )(mem2d, g, w_xkv)


def _rotary_lane_table():
    half = ROT_DIM // 2
    inv_freq = ROPE_THETA ** (-jnp.arange(0, ROT_DIM, 2, dtype=F32) / ROT_DIM)
    d = jnp.arange(LANES) % DIFF_HEAD_DIM
    freq = jnp.where(d < ROT_DIM, inv_freq[d % half], 0.0)
    lo = jnp.where(d < half, -1.0, 0.0)
    hi = jnp.where((d >= half) & (d < ROT_DIM), 1.0, 0.0)
    tab = jnp.zeros((SUBLANES, LANES), F32)
    return tab.at[0].set(freq).at[1].set(lo).at[2].set(hi)


def kernel(x, mem, positions, norm_mix_g, w_in, lam_q1, lam_k1, lam_q2, lam_k2, subln_g, conv_w,
           w_mix_out, norm_x_g, mem_norm_g, w_xq, w_xkv, w_xo, norm_ffn_g, w_ff1, w_ff2, final_g):
    b, s, d = x.shape
    aw = ATTN_WIDTH
    pos3 = positions.reshape(b, s, 1)
    rot_tab = _rotary_lane_table()
    kv = _mem_kv(mem.reshape(b * MEM_LEN, d), mem_norm_g.reshape(1, d), w_xkv.astype(BF16))
    kv = kv.reshape(DEPTH, b, MEM_LEN, 2 * d)
    h = x
    for l in range(DEPTH):
        lambda_init = 0.8 - 0.6 * math.exp(-0.3 * l)
        w_in_l = w_in[l].astype(BF16)
        q, k, vt, conv = _in_proj(h, pos3, rot_tab, norm_mix_g[l].reshape(1, d),
                                  w_in_l, w_in_l[:, 2 * aw:3 * aw].T, conv_w[l])
        lam_rows = jnp.stack([lam_q1[l], lam_k1[l], lam_q2[l], lam_k2[l]]).astype(F32)
        attn = _diff_attn(lam_rows, q, k, vt, subln_g[l].reshape(HEAD_W, 1), lambda_init)
        h = _mix_xattn(h, attn, conv, w_mix_out[l].astype(BF16), norm_x_g[l].reshape(1, d),
                       w_xq[l].astype(BF16), kv[l], w_xo[l].astype(BF16))
        h = _ffn(h.reshape(b * s, d), norm_ffn_g[l].reshape(1, d), w_ff1[l].astype(BF16),
                 w_ff2[l].astype(BF16), final_g.reshape(1, d), l == DEPTH - 1).reshape(b, s, d)
    return h
```

```python
import functools
import math

import jax
import jax.numpy as jnp
from jax import lax
from jax.experimental import pallas as pl
from jax.experimental.pallas import tpu as pltpu

D_MODEL = 1024
DEPTH = 2
MEM_LEN = 256
ATTN_WIDTH = D_MODEL // 2
CONV_WIDTH_CH = D_MODEL - ATTN_WIDTH
DIFF_HEAD_DIM = 64
DIFF_HEADS = ATTN_WIDTH // (2 * DIFF_HEAD_DIM)
HEAD_W = 2 * DIFF_HEAD_DIM
CONV_K = 3
ROT_DIM = DIFF_HEAD_DIM // 4
ROPE_THETA = 500000.0
X_HEADS = 4
X_HEAD_DIM = D_MODEL // X_HEADS
D_FF = 4 * D_MODEL
NEG_INF = -1e30
NORM_EPS = 1e-6
SUBLN_EPS = 1e-5

LANES = 128
SUBLANES = 8
VMEM_LIMIT = 56 * 1024 * 1024

ROW_TILE = 512
Q_TILE = 512
SUM_ROWS = 16
FF_CHUNK = 1024

BF16 = jnp.bfloat16
F32 = jnp.float32


def _rms(x, g, eps):
    return x * lax.rsqrt(jnp.mean(x * x, axis=-1, keepdims=True) + eps) * g


def _dot(a, b):
    return jnp.dot(a, b, preferred_element_type=F32)


def _dot_nt(a, b):
    return lax.dot_general(a, b, (((1,), (1,)), ((), ())), preferred_element_type=F32)


def _in_proj_kernel(pos_ref, inv_freq_ref, x_ref, g_ref, w_ref, wvt_ref, cw_ref,
                    q_ref, k_ref, vt_ref, c_ref, carry_ref):
    tm = x_ref.shape[1]
    aw = ATTN_WIDTH
    xn = _rms(x_ref[0], g_ref[...], NORM_EPS).astype(BF16)

    cwid = CONV_WIDTH_CH
    b_gate = _dot(xn, w_ref[:, 3 * aw:3 * aw + cwid])
    c_gate = _dot(xn, w_ref[:, 3 * aw + cwid:3 * aw + 2 * cwid])
    hc = _dot(xn, w_ref[:, 3 * aw + 2 * cwid:3 * aw + 3 * cwid])
    u = c_gate * hc

    @pl.when(pl.program_id(1) == 0)
    def _():
        carry_ref[...] = jnp.zeros_like(carry_ref)

    prev = carry_ref[...]
    row = lax.broadcasted_iota(jnp.int32, u.shape, 0)
    last = prev[SUBLANES - 1:SUBLANES, :]
    last2 = prev[SUBLANES - 2:SUBLANES - 1, :]
    u1 = jnp.where(row == 0, last, pltpu.roll(u, 1, axis=0))
    u2 = jnp.where(row == 0, last2, jnp.where(row == 1, last, pltpu.roll(u, 2, axis=0)))
    y = cw_ref[0:1, :] * u2 + cw_ref[1:2, :] * u1 + cw_ref[2:3, :] * u
    c_ref[0] = (b_gate * y).astype(BF16)
    carry_ref[...] = u[tm - SUBLANES:tm, :]

    half = ROT_DIM // 2
    ang_t = inv_freq_ref[...] * pos_ref[0].astype(F32)
    cos_t, sin_t = jnp.cos(ang_t), jnp.sin(ang_t)
    rest = DIFF_HEAD_DIM - ROT_DIM
    ones, zeros = jnp.ones((rest, tm), F32), jnp.zeros((rest, tm), F32)
    cos = jnp.concatenate([cos_t, cos_t, ones] * 2, axis=0).T
    sin = jnp.concatenate([-sin_t, sin_t, zeros] * 2, axis=0).T
    lane = lax.broadcasted_iota(jnp.int32, (tm, HEAD_W), 1)
    pairs_up = (lane & (DIFF_HEAD_DIM - 1)) < half

    def rotate(t):
        up = pltpu.roll(t, LANES - half, axis=1)
        dn = pltpu.roll(t, half, axis=1)
        return t * cos + jnp.where(pairs_up, up, dn) * sin

    scale = math.log2(math.e) * DIFF_HEAD_DIM ** -0.5
    q = _dot(xn, w_ref[:, 0:aw])
    k = _dot(xn, w_ref[:, aw:2 * aw])
    for h in range(DIFF_HEADS):
        lo, hi = h * HEAD_W, (h + 1) * HEAD_W
        q_ref[0, :, lo:hi] = (rotate(q[:, lo:hi]) * scale).astype(BF16)
        k_ref[0, :, lo:hi] = rotate(k[:, lo:hi]).astype(BF16)
    vt_ref[0, 0] = _dot_nt(wvt_ref[...], xn).astype(BF16)


def _in_proj(h, pos3, inv_freq, g, w_in, w_vt, conv_w):
    b, s, d = h.shape
    tm = ROW_TILE
    n_cols = w_in.shape[1]
    act = jax.ShapeDtypeStruct((b, s, ATTN_WIDTH), BF16)
    act_spec = pl.BlockSpec((1, tm, ATTN_WIDTH), lambda i, j: (i, j, 0))
    vt = jax.ShapeDtypeStruct((b, s // tm, ATTN_WIDTH, tm), BF16)
    vt_spec = pl.BlockSpec((1, 1, ATTN_WIDTH, tm), lambda i, j: (i, j, 0, 0))
    return pl.pallas_call(
        _in_proj_kernel,
        grid=(b, s // tm),
        in_specs=[
            pl.BlockSpec((1, 1, tm), lambda i, j: (i, 0, j)),
            pl.BlockSpec((ROT_DIM // 2, 1), lambda i, j: (0, 0)),
            pl.BlockSpec((1, tm, d), lambda i, j: (i, j, 0)),
            pl.BlockSpec((1, d), lambda i, j: (0, 0)),
            pl.BlockSpec((d, n_cols), lambda i, j: (0, 0)),
            pl.BlockSpec((ATTN_WIDTH, d), lambda i, j: (0, 0)),
            pl.BlockSpec((CONV_K, CONV_WIDTH_CH), lambda i, j: (0, 0)),
        ],
        out_specs=[act_spec, act_spec, vt_spec, act_spec],
        out_shape=[act, act, vt, act],
        scratch_shapes=[pltpu.VMEM((SUBLANES, CONV_WIDTH_CH), F32)],
        compiler_params=pltpu.CompilerParams(
            dimension_semantics=("arbitrary", "arbitrary"), vmem_limit_bytes=VMEM_LIMIT),
        name="in_proj",
    )(pos3, inv_freq, h, g, w_in, w_vt, conv_w)


def _diff_attn_kernel(lam_ref, q_ref, k_ref, vt_ref, g_ref, o_ref, *, out_scale, lambda_init):
    tq = Q_TILE
    tk = vt_ref.shape[3]
    n_q = q_ref.shape[1] // tq
    lam = (jnp.exp(jnp.sum(lam_ref[0:1, :] * lam_ref[1:2, :], axis=-1, keepdims=True))
           - jnp.exp(jnp.sum(lam_ref[2:3, :] * lam_ref[3:4, :], axis=-1, keepdims=True))
           + lambda_init)
    lane = lax.broadcasted_iota(jnp.int32, (tq, HEAD_W), 1)
    first = lane < DIFF_HEAD_DIM
    causal = (lax.broadcasted_iota(jnp.int32, (tk, tq), 0) <= lax.broadcasted_iota(jnp.int32, (tk, tq), 1))
    ones_rows = jnp.ones((SUM_ROWS, tk), BF16)

    pairs = [(qi, j) for qi in range(n_q) for j in range((qi * tq) // tk + 1)]

    def scores(qi, j):
        qh = q_ref[0, qi * tq:(qi + 1) * tq, :]
        zero = jnp.zeros_like(qh)
        kj = k_ref[0, j * tk:(j + 1) * tk, :]
        return (_dot_nt(kj, jnp.where(first, qh, zero)), _dot_nt(kj, jnp.where(first, zero, qh)))

    def finalize(qi, state):
        (_, l0, a0), (_, l1, a1) = state
        ot = a0 / l0 - lam * (a1 / l1)
        ms = jnp.mean(ot * ot, axis=0, keepdims=True)
        ot = ot * lax.rsqrt(ms + SUBLN_EPS) * g_ref[...] * out_scale
        o_ref[0, qi * tq:(qi + 1) * tq, :] = ot.T.astype(o_ref.dtype)

    s_next = scores(*pairs[0])
    state = None
    for t, (qi, j) in enumerate(pairs):
        s_cur = s_next
        if t + 1 < len(pairs):
            s_next = scores(*pairs[t + 1])
        diag = (j + 1) * tk >= (qi + 1) * tq
        vt_aug = jnp.concatenate([vt_ref[0, j], ones_rows], axis=0)
        new_state = []
        for c in range(2):
            s = s_cur[c]
            if diag:
                s = jnp.where(causal, s, NEG_INF)
            m_tile = jnp.max(s, axis=0, keepdims=True)
            if j == 0:
                m_new = m_tile
            else:
                m_old, l_old, acc_old = state[c]
                m_new = jnp.maximum(m_old, m_tile)
                alpha = jnp.exp2(m_old - m_new)
            pv = _dot(vt_aug, jnp.exp2(s - m_new).astype(BF16))
            acc, l = pv[:HEAD_W], pv[HEAD_W:HEAD_W + 1]
            if j > 0:
                acc, l = alpha * acc_old + acc, alpha * l_old + l
            new_state.append((m_new, l, acc))
        state = new_state
        if diag:
            finalize(qi, state)


def _diff_attn(lam_rows, q, k, vt, subln_g_col, lambda_init):
    b, s, w = q.shape
    nk, tk = vt.shape[1], vt.shape[3]
    assert Q_TILE == tk, "diagonal tiles are assumed square"
    kern = functools.partial(_diff_attn_kernel, out_scale=1.0 - lambda_init, lambda_init=lambda_init)
    return pl.pallas_call(
        kern,
        grid=(b, DIFF_HEADS),
        in_specs=[
            pl.BlockSpec((4, DIFF_HEAD_DIM), lambda i, h: (0, 0)),
            pl.BlockSpec((1, s, HEAD_W), lambda i, h: (i, 0, h)),
            pl.BlockSpec((1, s, HEAD_W), lambda i, h: (i, 0, h)),
            pl.BlockSpec((1, nk, HEAD_W, tk), lambda i, h: (i, 0, h, 0)),
            pl.BlockSpec((HEAD_W, 1), lambda i, h: (0, 0)),
        ],
        out_specs=pl.BlockSpec((1, s, HEAD_W), lambda i, h: (i, 0, h)),
        out_shape=jax.ShapeDtypeStruct((b, s, w), BF16),
        compiler_params=pltpu.CompilerParams(
            dimension_semantics=("arbitrary", "arbitrary"), vmem_limit_bytes=VMEM_LIMIT),
        name="diff_attn",
    )(lam_rows, q, k, vt, subln_g_col)


def _mix_xattn_kernel(h_ref, a_ref, c_ref, wmo_ref, g_ref, wq_ref, kv_ref, wo_ref, o_ref):
    aw = ATTN_WIDTH
    h1 = h_ref[0] + _dot(a_ref[0], wmo_ref[0:aw, :]) + _dot(c_ref[0], wmo_ref[aw:, :])
    xn = _rms(h1, g_ref[...], NORM_EPS).astype(BF16)
    scale = X_HEAD_DIM ** -0.5
    q = (_dot(xn, wq_ref[...]) * scale).astype(BF16)
    heads = []
    for hh in range(X_HEADS):
        lo, hi = hh * X_HEAD_DIM, (hh + 1) * X_HEAD_DIM
        s = _dot_nt(q[:, lo:hi], kv_ref[0, :, lo:hi])
        p = jnp.exp(s - jnp.max(s, axis=-1, keepdims=True))
        l = jnp.sum(p, axis=-1, keepdims=True)
        o = _dot(p.astype(BF16), kv_ref[0, :, D_MODEL + lo:D_MODEL + hi]) / l
        heads.append(o.astype(BF16))
    o_all = jnp.concatenate(heads, axis=-1)
    o_ref[0] = h1 + _dot(o_all, wo_ref[...])


def _mix_xattn(h, attn, conv, w_mo, g, w_xq, kv, w_xo):
    b, s, d = h.shape
    tm = ROW_TILE
    full = lambda shape: pl.BlockSpec(shape, lambda i, j: (0,) * len(shape))
    return pl.pallas_call(
        _mix_xattn_kernel,
        grid=(b, s // tm),
        in_specs=[
            pl.BlockSpec((1, tm, d), lambda i, j: (i, j, 0)),
            pl.BlockSpec((1, tm, ATTN_WIDTH), lambda i, j: (i, j, 0)),
            pl.BlockSpec((1, tm, CONV_WIDTH_CH), lambda i, j: (i, j, 0)),
            full((d, d)),
            full((1, d)),
            full((d, d)),
            pl.BlockSpec((1, MEM_LEN, 2 * d), lambda i, j: (i, 0, 0)),
            full((d, d)),
        ],
        out_specs=pl.BlockSpec((1, tm, d), lambda i, j: (i, j, 0)),
        out_shape=jax.ShapeDtypeStruct((b, s, d), F32),
        compiler_params=pltpu.CompilerParams(
            dimension_semantics=("arbitrary", "arbitrary"), vmem_limit_bytes=VMEM_LIMIT),
        name="mix_xattn",
    )(h, attn, conv, w_mo, g, w_xq, kv, w_xo)


def _ffn_kernel(h_ref, g_ref, w1_ref, w2_ref, fg_ref, o_ref, acc_ref, *, final_norm):
    h = h_ref[...]
    xn = _rms(h, g_ref[...], NORM_EPS).astype(BF16)
    for c in range(D_FF // FF_CHUNK):
        lo, hi = c * FF_CHUNK, (c + 1) * FF_CHUNK
        f = jnp.square(jnp.maximum(_dot(xn, w1_ref[:, lo:hi]), 0.0)).astype(BF16)
        part = _dot(f, w2_ref[lo:hi, :])
        if c == 0:
            acc_ref[...] = part
        else:
            acc_ref[...] += part
    out = h + acc_ref[...]
    if final_norm:
        out = _rms(out, fg_ref[...], NORM_EPS)
    o_ref[...] = out


def _ffn(h2d, g, w1, w2, final_g, final_norm):
    n, d = h2d.shape
    tm = ROW_TILE
    full = lambda shape: pl.BlockSpec(shape, lambda i: (0,) * len(shape))
    return pl.pallas_call(
        functools.partial(_ffn_kernel, final_norm=final_norm),
        grid=(n // tm,),
        in_specs=[
            pl.BlockSpec((tm, d), lambda i: (i, 0)),
            full((1, d)),
            full((d, D_FF)),
            full((D_FF, d)),
            full((1, d)),
        ],
        out_specs=pl.BlockSpec((tm, d), lambda i: (i, 0)),
        out_shape=jax.ShapeDtypeStruct((n, d), F32),
        scratch_shapes=[pltpu.VMEM((tm, d), F32)],
        compiler_params=pltpu.CompilerParams(
            dimension_semantics=("arbitrary",), vmem_limit_bytes=VMEM_LIMIT),
        name="ffn",
    )(h2d, g, w1, w2, final_g)


def _mem_kv_kernel(m_ref, g_ref, w_ref, o_ref):
    mn = _rms(m_ref[...], g_ref[...], NORM_EPS).astype(BF16)
    o_ref[0] = _dot(mn, w_ref[0]).astype(BF16)


def _mem_kv(mem2d, g, w_xkv):
    n, d = mem2d.shape
    depth, _, n_cols = w_xkv.shape
    tm = ROW_TILE
    return pl.pallas_call(
        _mem_kv_kernel,
        grid=(depth, n // tm),
        in_specs=[
            pl.BlockSpec((tm, d), lambda l, i: (i, 0)),
            pl.BlockSpec((1, d), lambda l, i: (0, 0)),
            pl.BlockSpec((1, d, n_cols), lambda l, i: (l, 0, 0)),
        ],
        out_specs=pl.BlockSpec((1, tm, n_cols), lambda l, i: (l, i, 0)),
        out_shape=jax.ShapeDtypeStruct((depth, n, n_cols), BF16),
        compiler_params=pltpu.CompilerParams(
            dimension_semantics=("arbitrary", "arbitrary"), vmem_limit_bytes=VMEM_LIMIT),
        name="mem_kv",
    )(mem2d, g, w_xkv)


def kernel(x, mem, positions, norm_mix_g, w_in, lam_q1, lam_k1, lam_q2, lam_k2, subln_g, conv_w,
           w_mix_out, norm_x_g, mem_norm_g, w_xq, w_xkv, w_xo, norm_ffn_g, w_ff1, w_ff2, final_g):
    b, s, d = x.shape
    aw = ATTN_WIDTH
    pos3 = positions.reshape(b, 1, s)
    inv_freq = (ROPE_THETA ** (-jnp.arange(0, ROT_DIM, 2, dtype=F32) / ROT_DIM)).reshape(ROT_DIM // 2, 1)
    kv = _mem_kv(mem.reshape(b * MEM_LEN, d), mem_norm_g.reshape(1, d), w_xkv.astype(BF16))
    kv = kv.reshape(DEPTH, b, MEM_LEN, 2 * d)
    h = x
    for l in range(DEPTH):
        lambda_init = 0.8 - 0.6 * math.exp(-0.3 * l)
        w_in_l = w_in[l].astype(BF16)
        q, k, vt, conv = _in_proj(h, pos3, inv_freq, norm_mix_g[l].reshape(1, d),
                                  w_in_l, w_in_l[:, 2 * aw:3 * aw].T, conv_w[l])
        lam_rows = jnp.stack([lam_q1[l], lam_k1[l], lam_q2[l], lam_k2[l]]).astype(F32)
        attn = _diff_attn(lam_rows, q, k, vt, subln_g[l].reshape(HEAD_W, 1), lambda_init)
        h = _mix_xattn(h, attn, conv, w_mix_out[l].astype(BF16), norm_x_g[l].reshape(1, d),
                       w_xq[l].astype(BF16), kv[l], w_xo[l].astype(BF16))
        h = _ffn(h.reshape(b * s, d), norm_ffn_g[l].reshape(1, d), w_ff1[l].astype(BF16),
                 w_ff2[l].astype(BF16), final_g.reshape(1, d), l == DEPTH - 1).reshape(b, s, d)
    return h
```

```python
import functools
import math

import jax
import jax.numpy as jnp
from jax import lax
from jax.experimental import pallas as pl
from jax.experimental.pallas import tpu as pltpu

D_MODEL = 1024
DEPTH = 2
MEM_LEN = 256
ATTN_WIDTH = D_MODEL // 2
CONV_WIDTH_CH = D_MODEL - ATTN_WIDTH
DIFF_HEAD_DIM = 64
DIFF_HEADS = ATTN_WIDTH // (2 * DIFF_HEAD_DIM)
HEAD_W = 2 * DIFF_HEAD_DIM
CONV_K = 3
ROT_DIM = DIFF_HEAD_DIM // 4
ROPE_THETA = 500000.0
X_HEADS = 4
X_HEAD_DIM = D_MODEL // X_HEADS
D_FF = 4 * D_MODEL
NEG_INF = -1e30
NORM_EPS = 1e-6
SUBLN_EPS = 1e-5

LANES = 128
SUBLANES = 8
VMEM_LIMIT = 56 * 1024 * 1024

ROW_TILE = 1024
Q_TILE = 512
SUM_ROWS = 16
FF_CHUNK = 1024

BF16 = jnp.bfloat16
F32 = jnp.float32


def _rms(x, g, eps):
    return x * lax.rsqrt(jnp.mean(x * x, axis=-1, keepdims=True) + eps) * g


def _dot(a, b):
    return jnp.dot(a, b, preferred_element_type=F32)


def _dot_nt(a, b):
    return lax.dot_general(a, b, (((1,), (1,)), ((), ())), preferred_element_type=F32)


def _resident(shape):
    zeros = (0,) * len(shape)
    return pl.BlockSpec(shape, lambda *_: zeros, pipeline_mode=pl.Buffered(1))


def _in_proj_kernel(pos_ref, inv_freq_ref, x_ref, g_ref, w_ref, wvt_ref, cw_ref,
                    q_ref, k_ref, vt_ref, c_ref, carry_ref):
    tm = x_ref.shape[1]
    aw = ATTN_WIDTH
    @pl.when(pl.program_id(1) == 0)
    def _():
        carry_ref[...] = jnp.zeros_like(carry_ref)

    xn = _rms(x_ref[0], g_ref[...], NORM_EPS).astype(BF16)

    half = ROT_DIM // 2
    ang_t = inv_freq_ref[...] * pos_ref[0].astype(F32)
    cos_t, sin_t = jnp.cos(ang_t), jnp.sin(ang_t)
    rest = DIFF_HEAD_DIM - ROT_DIM
    ones, zeros = jnp.ones((rest, tm), F32), jnp.zeros((rest, tm), F32)
    cos = jnp.concatenate([cos_t, cos_t, ones] * 2, axis=0).T
    sin = jnp.concatenate([-sin_t, sin_t, zeros] * 2, axis=0).T
    lane = lax.broadcasted_iota(jnp.int32, (tm, HEAD_W), 1)
    pairs_up = (lane & (DIFF_HEAD_DIM - 1)) < half

    def rotate(t):
        up = pltpu.roll(t, LANES - half, axis=1)
        dn = pltpu.roll(t, half, axis=1)
        return t * cos + jnp.where(pairs_up, up, dn) * sin

    scale = math.log2(math.e) * DIFF_HEAD_DIM ** -0.5
    q = _dot(xn, w_ref[:, 0:aw])
    k = _dot(xn, w_ref[:, aw:2 * aw])
    for h in range(DIFF_HEADS):
        lo, hi = h * HEAD_W, (h + 1) * HEAD_W
        q_ref[0, :, lo:hi] = (rotate(q[:, lo:hi]) * scale).astype(BF16)
        k_ref[0, :, lo:hi] = rotate(k[:, lo:hi]).astype(BF16)

    cwid = CONV_WIDTH_CH
    c_gate = _dot(xn, w_ref[:, 3 * aw + cwid:3 * aw + 2 * cwid])
    hc = _dot(xn, w_ref[:, 3 * aw + 2 * cwid:3 * aw + 3 * cwid])
    b_gate = _dot(xn, w_ref[:, 3 * aw:3 * aw + cwid])
    u = c_gate * hc
    prev = carry_ref[...]
    row = lax.broadcasted_iota(jnp.int32, u.shape, 0)
    last = prev[SUBLANES - 1:SUBLANES, :]
    last2 = prev[SUBLANES - 2:SUBLANES - 1, :]
    u1 = jnp.where(row == 0, last, pltpu.roll(u, 1, axis=0))
    u2 = jnp.where(row == 0, last2, jnp.where(row == 1, last, pltpu.roll(u, 2, axis=0)))
    y = cw_ref[0:1, :] * u2 + cw_ref[1:2, :] * u1 + cw_ref[2:3, :] * u
    c_ref[0] = (b_gate * y).astype(BF16)
    carry_ref[...] = u[tm - SUBLANES:tm, :]

    vt = _dot_nt(wvt_ref[...], xn).astype(BF16)
    tk = vt_ref.shape[3]
    for i in range(tm // tk):
        vt_ref[0, i] = vt[:, i * tk:(i + 1) * tk]


def _in_proj(h, pos3, inv_freq, g, w_in, w_vt, conv_w):
    b, s, d = h.shape
    tm = ROW_TILE
    n_cols = w_in.shape[1]
    act = jax.ShapeDtypeStruct((b, s, ATTN_WIDTH), BF16)
    act_spec = pl.BlockSpec((1, tm, ATTN_WIDTH), lambda i, j: (i, j, 0))
    tk = Q_TILE
    vt = jax.ShapeDtypeStruct((b, s // tk, ATTN_WIDTH, tk), BF16)
    vt_spec = pl.BlockSpec((1, tm // tk, ATTN_WIDTH, tk), lambda i, j: (i, j, 0, 0))
    return pl.pallas_call(
        _in_proj_kernel,
        grid=(b, s // tm),
        in_specs=[
            pl.BlockSpec((1, 1, tm), lambda i, j: (i, 0, j)),
            _resident((ROT_DIM // 2, 1)),
            pl.BlockSpec((1, tm, d), lambda i, j: (i, j, 0)),
            _resident((1, d)),
            _resident((d, n_cols)),
            _resident((ATTN_WIDTH, d)),
            _resident((CONV_K, CONV_WIDTH_CH)),
        ],
        out_specs=[act_spec, act_spec, vt_spec, act_spec],
        out_shape=[act, act, vt, act],
        scratch_shapes=[pltpu.VMEM((SUBLANES, CONV_WIDTH_CH), F32)],
        compiler_params=pltpu.CompilerParams(
            dimension_semantics=("arbitrary", "arbitrary"), vmem_limit_bytes=VMEM_LIMIT),
        name="in_proj",
    )(pos3, inv_freq, h, g, w_in, w_vt, conv_w)


def _diff_attn_kernel(lam_ref, q_ref, k_ref, vt_ref, g_ref, o_ref, *, out_scale, lambda_init):
    tq = Q_TILE
    tk = vt_ref.shape[3]
    n_q = q_ref.shape[1] // tq
    lam = (jnp.exp(jnp.sum(lam_ref[0:1, :] * lam_ref[1:2, :], axis=-1, keepdims=True))
           - jnp.exp(jnp.sum(lam_ref[2:3, :] * lam_ref[3:4, :], axis=-1, keepdims=True))
           + lambda_init)
    lane = lax.broadcasted_iota(jnp.int32, (tq, HEAD_W), 1)
    first = lane < DIFF_HEAD_DIM
    causal = (lax.broadcasted_iota(jnp.int32, (tk, tq), 0) <= lax.broadcasted_iota(jnp.int32, (tk, tq), 1))
    ones_rows = jnp.ones((SUM_ROWS, tk), BF16)

    pairs = [(qi, j) for qi in range(n_q) for j in range((qi * tq) // tk + 1)]

    def scores(qi, j):
        qh = q_ref[0, qi * tq:(qi + 1) * tq, :]
        zero = jnp.zeros_like(qh)
        kj = k_ref[0, j * tk:(j + 1) * tk, :]
        return (_dot_nt(kj, jnp.where(first, qh, zero)), _dot_nt(kj, jnp.where(first, zero, qh)))

    def finalize(qi, state):
        (_, l0, a0), (_, l1, a1) = state
        ot = a0 / l0 - lam * (a1 / l1)
        ms = jnp.mean(ot * ot, axis=0, keepdims=True)
        ot = ot * lax.rsqrt(ms + SUBLN_EPS) * g_ref[...] * out_scale
        o_ref[0, qi * tq:(qi + 1) * tq, :] = ot.T.astype(o_ref.dtype)

    s_next = scores(*pairs[0])
    state = None
    for t, (qi, j) in enumerate(pairs):
        s_cur = s_next
        if t + 1 < len(pairs):
            s_next = scores(*pairs[t + 1])
        diag = (j + 1) * tk >= (qi + 1) * tq
        vt_aug = jnp.concatenate([vt_ref[0, j], ones_rows], axis=0)
        new_state = []
        for c in range(2):
            s = s_cur[c]
            if diag:
                s = jnp.where(causal, s, NEG_INF)
            m_tile = jnp.max(s, axis=0, keepdims=True)
            if j == 0:
                m_new = m_tile
            else:
                m_old, l_old, acc_old = state[c]
                m_new = jnp.maximum(m_old, m_tile)
                alpha = jnp.exp2(m_old - m_new)
            pv = _dot(vt_aug, jnp.exp2(s - m_new).astype(BF16))
            acc, l = pv[:HEAD_W], pv[HEAD_W:HEAD_W + 1]
            if j > 0:
                acc, l = alpha * acc_old + acc, alpha * l_old + l
            new_state.append((m_new, l, acc))
        state = new_state
        if diag:
            finalize(qi, state)


def _diff_attn(lam_rows, q, k, vt, subln_g_col, lambda_init):
    b, s, w = q.shape
    nk, tk = vt.shape[1], vt.shape[3]
    assert Q_TILE == tk, "diagonal tiles are assumed square"
    kern = functools.partial(_diff_attn_kernel, out_scale=1.0 - lambda_init, lambda_init=lambda_init)
    return pl.pallas_call(
        kern,
        grid=(b, DIFF_HEADS),
        in_specs=[
            pl.BlockSpec((4, DIFF_HEAD_DIM), lambda i, h: (0, 0)),
            pl.BlockSpec((1, s, HEAD_W), lambda i, h: (i, 0, h)),
            pl.BlockSpec((1, s, HEAD_W), lambda i, h: (i, 0, h)),
            pl.BlockSpec((1, nk, HEAD_W, tk), lambda i, h: (i, 0, h, 0)),
            pl.BlockSpec((HEAD_W, 1), lambda i, h: (0, 0)),
        ],
        out_specs=pl.BlockSpec((1, s, HEAD_W), lambda i, h: (i, 0, h)),
        out_shape=jax.ShapeDtypeStruct((b, s, w), BF16),
        compiler_params=pltpu.CompilerParams(
            dimension_semantics=("arbitrary", "arbitrary"), vmem_limit_bytes=VMEM_LIMIT),
        name="diff_attn",
    )(lam_rows, q, k, vt, subln_g_col)


def _mix_xattn_kernel(h_ref, a_ref, c_ref, wmo_ref, g_ref, wq_ref, kv_ref, wo_ref, o_ref):
    aw = ATTN_WIDTH
    scale = X_HEAD_DIM ** -0.5
    h1 = h_ref[0] + _dot(a_ref[0], wmo_ref[0:aw, :]) + _dot(c_ref[0], wmo_ref[aw:, :])
    xn = _rms(h1, g_ref[...], NORM_EPS).astype(BF16)
    q = (_dot(xn, wq_ref[...]) * scale).astype(BF16)
    heads = []
    for hh in range(X_HEADS):
        lo, hi = hh * X_HEAD_DIM, (hh + 1) * X_HEAD_DIM
        s = _dot_nt(q[:, lo:hi], kv_ref[0, 0, :, lo:hi])
        p = jnp.exp(s - jnp.max(s, axis=-1, keepdims=True))
        l = jnp.sum(p, axis=-1, keepdims=True)
        o = _dot(p.astype(BF16), kv_ref[0, 0, :, D_MODEL + lo:D_MODEL + hi]) / l
        heads.append(o.astype(BF16))
    o_all = jnp.concatenate(heads, axis=-1)
    o_ref[0] = h1 + _dot(o_all, wo_ref[...])


def _mix_xattn(h, attn, conv, w_mo, g, w_xq, kv, layer, w_xo):
    b, s, d = h.shape
    tm = ROW_TILE
    return pl.pallas_call(
        _mix_xattn_kernel,
        grid=(b, s // tm),
        in_specs=[
            pl.BlockSpec((1, tm, d), lambda i, j: (i, j, 0)),
            pl.BlockSpec((1, tm, ATTN_WIDTH), lambda i, j: (i, j, 0)),
            pl.BlockSpec((1, tm, CONV_WIDTH_CH), lambda i, j: (i, j, 0)),
            _resident((d, d)),
            _resident((1, d)),
            _resident((d, d)),
            pl.BlockSpec((1, 1, MEM_LEN, 2 * d), lambda i, j: (layer, i, 0, 0)),
            _resident((d, d)),
        ],
        out_specs=pl.BlockSpec((1, tm, d), lambda i, j: (i, j, 0)),
        out_shape=jax.ShapeDtypeStruct((b, s, d), F32),
        compiler_params=pltpu.CompilerParams(
            dimension_semantics=("arbitrary", "arbitrary"), vmem_limit_bytes=VMEM_LIMIT),
        name="mix_xattn",
    )(h, attn, conv, w_mo, g, w_xq, kv, w_xo)


def _ffn_kernel(h_ref, g_ref, w1_ref, w2_ref, fg_ref, o_ref, acc_ref, *, final_norm):
    h = h_ref[...]
    xn = _rms(h, g_ref[...], NORM_EPS).astype(BF16)
    for c in range(D_FF // FF_CHUNK):
        lo, hi = c * FF_CHUNK, (c + 1) * FF_CHUNK
        f = jnp.square(jnp.maximum(_dot(xn, w1_ref[:, lo:hi]), 0.0)).astype(BF16)
        part = _dot(f, w2_ref[lo:hi, :])
        if c == 0:
            acc_ref[...] = part
        else:
            acc_ref[...] += part
    out = h + acc_ref[...]
    if final_norm:
        out = _rms(out, fg_ref[...], NORM_EPS)
    o_ref[...] = out


def _ffn(h2d, g, w1, w2, final_g, final_norm):
    n, d = h2d.shape
    tm = ROW_TILE
    return pl.pallas_call(
        functools.partial(_ffn_kernel, final_norm=final_norm),
        grid=(n // tm,),
        in_specs=[
            pl.BlockSpec((tm, d), lambda i: (i, 0)),
            _resident((1, d)),
            _resident((d, D_FF)),
            _resident((D_FF, d)),
            _resident((1, d)),
        ],
        out_specs=pl.BlockSpec((tm, d), lambda i: (i, 0)),
        out_shape=jax.ShapeDtypeStruct((n, d), F32),
        scratch_shapes=[pltpu.VMEM((tm, d), F32)],
        compiler_params=pltpu.CompilerParams(
            dimension_semantics=("arbitrary",), vmem_limit_bytes=VMEM_LIMIT),
        name="ffn",
    )(h2d, g, w1, w2, final_g)


def _mem_kv_kernel(m_ref, g_ref, w_ref, o_ref):
    mn = _rms(m_ref[...], g_ref[...], NORM_EPS).astype(BF16)
    o_ref[0] = _dot(mn, w_ref[0]).astype(BF16)


def _mem_kv(mem2d, g, w_xkv):
    n, d = mem2d.shape
    depth, _, n_cols = w_xkv.shape
    tm = ROW_TILE
    return pl.pallas_call(
        _mem_kv_kernel,
        grid=(depth, n // tm),
        in_specs=[
            pl.BlockSpec((tm, d), lambda l, i: (i, 0)),
            pl.BlockSpec((1, d), lambda l, i: (0, 0)),
            pl.BlockSpec((1, d, n_cols), lambda l, i: (l, 0, 0)),
        ],
        out_specs=pl.BlockSpec((1, tm, n_cols), lambda l, i: (l, i, 0)),
        out_shape=jax.ShapeDtypeStruct((depth, n, n_cols), BF16),
        compiler_params=pltpu.CompilerParams(
            dimension_semantics=("arbitrary", "arbitrary"), vmem_limit_bytes=VMEM_LIMIT),
        name="mem_kv",
    )(mem2d, g, w_xkv)


def kernel(x, mem, positions, norm_mix_g, w_in, lam_q1, lam_k1, lam_q2, lam_k2, subln_g, conv_w,
           w_mix_out, norm_x_g, mem_norm_g, w_xq, w_xkv, w_xo, norm_ffn_g, w_ff1, w_ff2, final_g):
    b, s, d = x.shape
    aw = ATTN_WIDTH
    pos3 = positions.reshape(b, 1, s)
    inv_freq = (ROPE_THETA ** (-jnp.arange(0, ROT_DIM, 2, dtype=F32) / ROT_DIM)).reshape(ROT_DIM // 2, 1)
    kv = _mem_kv(mem.reshape(b * MEM_LEN, d), mem_norm_g.reshape(1, d), w_xkv.astype(BF16))
    kv = kv.reshape(DEPTH, b, MEM_LEN, 2 * d)
    h = x
    for l in range(DEPTH):
        lambda_init = 0.8 - 0.6 * math.exp(-0.3 * l)
        w_in_l = w_in[l].astype(BF16)
        q, k, vt, conv = _in_proj(h, pos3, inv_freq, norm_mix_g[l].reshape(1, d),
                                  w_in_l, w_in_l[:, 2 * aw:3 * aw].T, conv_w[l])
        lam_rows = jnp.stack([lam_q1[l], lam_k1[l], lam_q2[l], lam_k2[l]]).astype(F32)
        attn = _diff_attn(lam_rows, q, k, vt, subln_g[l].reshape(HEAD_W, 1), lambda_init)
        h = _mix_xattn(h, attn, conv, w_mix_out[l].astype(BF16), norm_x_g[l].reshape(1, d),
                       w_xq[l].astype(BF16), kv, l, w_xo[l].astype(BF16))
        h = _ffn(h.reshape(b * s, d), norm_ffn_g[l].reshape(1, d), w_ff1[l].astype(BF16),
                 w_ff2[l].astype(BF16), final_g.reshape(1, d), l == DEPTH - 1).reshape(b, s, d)
    return h
```

```python
import functools
import math

import jax
import jax.numpy as jnp
from jax import lax
from jax.experimental import pallas as pl
from jax.experimental.pallas import tpu as pltpu

D_MODEL = 1024
DEPTH = 2
MEM_LEN = 256
ATTN_WIDTH = D_MODEL // 2
CONV_WIDTH_CH = D_MODEL - ATTN_WIDTH
DIFF_HEAD_DIM = 64
DIFF_HEADS = ATTN_WIDTH // (2 * DIFF_HEAD_DIM)
HEAD_W = 2 * DIFF_HEAD_DIM
CONV_K = 3
ROT_DIM = DIFF_HEAD_DIM // 4
ROPE_THETA = 500000.0
X_HEADS = 4
X_HEAD_DIM = D_MODEL // X_HEADS
D_FF = 4 * D_MODEL
NEG_INF = -1e30
NORM_EPS = 1e-6
SUBLN_EPS = 1e-5

LANES = 128
SUBLANES = 8
VMEM_LIMIT = 56 * 1024 * 1024

ROW_TILE = 1024
Q_TILE = 512
SUM_ROWS = 16
FF_CHUNK = 1024

BF16 = jnp.bfloat16
F32 = jnp.float32


def _rms(x, g, eps):
    return x * lax.rsqrt(jnp.mean(x * x, axis=-1, keepdims=True) + eps) * g


def _dot(a, b):
    return jnp.dot(a, b, preferred_element_type=F32)


def _dot_nt(a, b):
    return lax.dot_general(a, b, (((1,), (1,)), ((), ())), preferred_element_type=F32)


def _resident(shape):
    zeros = (0,) * len(shape)
    return pl.BlockSpec(shape, lambda *_: zeros, pipeline_mode=pl.Buffered(1))


def _layer_resident(shape, layer):
    index = (layer,) + (0,) * len(shape)
    return pl.BlockSpec((1,) + tuple(shape), lambda *_: index, pipeline_mode=pl.Buffered(1))


def _in_proj_kernel(pos_ref, inv_freq_ref, x_ref, g_ref, w_ref, wvt_ref, cw_ref,
                    q_ref, k_ref, vt_ref, c_ref, carry_ref):
    tm = x_ref.shape[1]
    aw = ATTN_WIDTH

    @pl.when(pl.program_id(1) == 0)
    def _():
        carry_ref[...] = jnp.zeros_like(carry_ref)

    xn = _rms(x_ref[0], g_ref[0], NORM_EPS).astype(BF16)

    half = ROT_DIM // 2
    ang_t = inv_freq_ref[...] * pos_ref[0].astype(F32)
    cos_t, sin_t = jnp.cos(ang_t), jnp.sin(ang_t)
    rest = DIFF_HEAD_DIM - ROT_DIM
    ones, zeros = jnp.ones((rest, tm), F32), jnp.zeros((rest, tm), F32)
    cos = jnp.concatenate([cos_t, cos_t, ones] * 2, axis=0).T
    sin = jnp.concatenate([-sin_t, sin_t, zeros] * 2, axis=0).T
    lane = lax.broadcasted_iota(jnp.int32, (tm, HEAD_W), 1)
    pairs_up = (lane & (DIFF_HEAD_DIM - 1)) < half

    def rotate(t):
        up = pltpu.roll(t, LANES - half, axis=1)
        dn = pltpu.roll(t, half, axis=1)
        return t * cos + jnp.where(pairs_up, up, dn) * sin

    scale = math.log2(math.e) * DIFF_HEAD_DIM ** -0.5
    q = _dot(xn, w_ref[0, :, 0:aw])
    k = _dot(xn, w_ref[0, :, aw:2 * aw])
    for h in range(DIFF_HEADS):
        lo, hi = h * HEAD_W, (h + 1) * HEAD_W
        q_ref[0, :, lo:hi] = (rotate(q[:, lo:hi]) * scale).astype(BF16)
        k_ref[0, :, lo:hi] = rotate(k[:, lo:hi]).astype(BF16)

    cwid = CONV_WIDTH_CH
    c_gate = _dot(xn, w_ref[0, :, 3 * aw + cwid:3 * aw + 2 * cwid])
    hc = _dot(xn, w_ref[0, :, 3 * aw + 2 * cwid:3 * aw + 3 * cwid])
    b_gate = _dot(xn, w_ref[0, :, 3 * aw:3 * aw + cwid])
    u = c_gate * hc
    prev = carry_ref[...]
    row = lax.broadcasted_iota(jnp.int32, u.shape, 0)
    last = prev[SUBLANES - 1:SUBLANES, :]
    last2 = prev[SUBLANES - 2:SUBLANES - 1, :]
    u1 = jnp.where(row == 0, last, pltpu.roll(u, 1, axis=0))
    u2 = jnp.where(row == 0, last2, jnp.where(row == 1, last, pltpu.roll(u, 2, axis=0)))
    cw = cw_ref[0]
    y = cw[0:1, :] * u2 + cw[1:2, :] * u1 + cw[2:3, :] * u
    c_ref[0] = (b_gate * y).astype(BF16)
    carry_ref[...] = u[tm - SUBLANES:tm, :]

    vt = _dot_nt(wvt_ref[0], xn).astype(BF16)
    tk = vt_ref.shape[3]
    for i in range(tm // tk):
        vt_ref[0, i] = vt[:, i * tk:(i + 1) * tk]


def _in_proj(h, pos3, inv_freq, g, w_in, w_vt, conv_w, layer):
    b, s, d = h.shape
    tm = ROW_TILE
    n_cols = w_in.shape[2]
    act = jax.ShapeDtypeStruct((b, s, ATTN_WIDTH), BF16)
    act_spec = pl.BlockSpec((1, tm, ATTN_WIDTH), lambda i, j: (i, j, 0))
    tk = Q_TILE
    vt = jax.ShapeDtypeStruct((b, s // tk, ATTN_WIDTH, tk), BF16)
    vt_spec = pl.BlockSpec((1, tm // tk, ATTN_WIDTH, tk), lambda i, j: (i, j, 0, 0))
    return pl.pallas_call(
        _in_proj_kernel,
        grid=(b, s // tm),
        in_specs=[
            pl.BlockSpec((1, 1, tm), lambda i, j: (i, 0, j)),
            _resident((ROT_DIM // 2, 1)),
            pl.BlockSpec((1, tm, d), lambda i, j: (i, j, 0)),
            _layer_resident((1, d), layer),
            _layer_resident((d, n_cols), layer),
            _layer_resident((ATTN_WIDTH, d), layer),
            _layer_resident((CONV_K, CONV_WIDTH_CH), layer),
        ],
        out_specs=[act_spec, act_spec, vt_spec, act_spec],
        out_shape=[act, act, vt, act],
        scratch_shapes=[pltpu.VMEM((SUBLANES, CONV_WIDTH_CH), F32)],
        compiler_params=pltpu.CompilerParams(
            dimension_semantics=("arbitrary", "arbitrary"), vmem_limit_bytes=VMEM_LIMIT),
        name="in_proj",
    )(pos3, inv_freq, h, g, w_in, w_vt, conv_w)


def _diff_attn_kernel(lam_ref, q_ref, k_ref, vt_ref, g_ref, o_ref, *, out_scale, lambda_init):
    tq = Q_TILE
    tk = vt_ref.shape[3]
    n_q = q_ref.shape[1] // tq
    lam_rows = lam_ref[0]
    lam = (jnp.exp(jnp.sum(lam_rows[0:1, :] * lam_rows[1:2, :], axis=-1, keepdims=True))
           - jnp.exp(jnp.sum(lam_rows[2:3, :] * lam_rows[3:4, :], axis=-1, keepdims=True))
           + lambda_init)
    g_col = g_ref[0]
    ones_rows = jnp.ones((SUM_ROWS, tk), BF16)
    hq = tq // 2

    def pieces(qi, j):
        if (j + 1) * tk >= (qi + 1) * tq:
            return [(0, hq, hq, True), (hq, tq - hq, tk, True)]
        return [(0, tq, tk, False)]

    def scores(qi, j):
        out = []
        for c0, nc, nk, _ in pieces(qi, j):
            qh = q_ref[0, qi * tq + c0:qi * tq + c0 + nc, :]
            first = lax.broadcasted_iota(jnp.int32, qh.shape, 1) < DIFF_HEAD_DIM
            zero = jnp.zeros_like(qh)
            kj = k_ref[0, j * tk:j * tk + nk, :]
            out.append((_dot_nt(kj, jnp.where(first, qh, zero)), _dot_nt(kj, jnp.where(first, zero, qh))))
        return out

    def finalize(row0, state):
        (_, l0, a0), (_, l1, a1) = state
        ot = a0 / l0 - lam * (a1 / l1)
        ms = jnp.mean(ot * ot, axis=0, keepdims=True)
        ot = ot * lax.rsqrt(ms + SUBLN_EPS) * g_col * out_scale
        o_ref[0, row0:row0 + ot.shape[1], :] = ot.T.astype(o_ref.dtype)

    pairs = [(qi, j) for qi in range(n_q) for j in range((qi * tq) // tk + 1)]
    s_next = scores(*pairs[0])
    state = None
    for t, (qi, j) in enumerate(pairs):
        s_cur = s_next
        if t + 1 < len(pairs):
            s_next = scores(*pairs[t + 1])
        vt_aug = jnp.concatenate([vt_ref[0, j], ones_rows], axis=0)
        new_state = []
        for (c0, nc, nk, masked), s_piece in zip(pieces(qi, j), s_cur):
            piece_state = []
            for c in range(2):
                s = s_piece[c]
                if masked:
                    key = lax.broadcasted_iota(jnp.int32, (nk, nc), 0)
                    qry = lax.broadcasted_iota(jnp.int32, (nk, nc), 1) + c0
                    s = jnp.where(key <= qry, s, NEG_INF)
                m_new = jnp.max(s, axis=0, keepdims=True)
                if j > 0:
                    m_old, l_old, acc_old = (a[:, c0:c0 + nc] for a in state[c])
                    m_new = jnp.maximum(m_old, m_new)
                    alpha = jnp.exp2(m_old - m_new)
                pv = _dot(vt_aug[:, 0:nk], jnp.exp2(s - m_new).astype(BF16))
                acc, l = pv[:HEAD_W], pv[HEAD_W:HEAD_W + 1]
                if j > 0:
                    acc, l = alpha * acc_old + acc, alpha * l_old + l
                piece_state.append((m_new, l, acc))
            if masked:
                finalize(qi * tq + c0, piece_state)
            else:
                new_state = piece_state
        state = new_state


def _diff_attn(lam_rows, q, k, vt, subln_g_col, lambda_init, layer):
    b, s, w = q.shape
    nk, tk = vt.shape[1], vt.shape[3]
    assert Q_TILE == tk, "diagonal tiles are assumed square"
    kern = functools.partial(_diff_attn_kernel, out_scale=1.0 - lambda_init, lambda_init=lambda_init)
    return pl.pallas_call(
        kern,
        grid=(b, DIFF_HEADS),
        in_specs=[
            _layer_resident((4, DIFF_HEAD_DIM), layer),
            pl.BlockSpec((1, s, HEAD_W), lambda i, h: (i, 0, h)),
            pl.BlockSpec((1, s, HEAD_W), lambda i, h: (i, 0, h)),
            pl.BlockSpec((1, nk, HEAD_W, tk), lambda i, h: (i, 0, h, 0)),
            _layer_resident((HEAD_W, 1), layer),
        ],
        out_specs=pl.BlockSpec((1, s, HEAD_W), lambda i, h: (i, 0, h)),
        out_shape=jax.ShapeDtypeStruct((b, s, w), BF16),
        compiler_params=pltpu.CompilerParams(
            dimension_semantics=("arbitrary", "arbitrary"), vmem_limit_bytes=VMEM_LIMIT),
        name="diff_attn",
    )(lam_rows, q, k, vt, subln_g_col)


def _mix_xattn_kernel(h_ref, a_ref, c_ref, wmo_ref, g_ref, wq_ref, kv_ref, wo_ref, o_ref):
    aw = ATTN_WIDTH
    scale = X_HEAD_DIM ** -0.5
    h1 = h_ref[0] + _dot(a_ref[0], wmo_ref[0, 0:aw, :]) + _dot(c_ref[0], wmo_ref[0, aw:, :])
    xn = _rms(h1, g_ref[0], NORM_EPS).astype(BF16)
    q = (_dot(xn, wq_ref[0]) * scale).astype(BF16)
    heads = []
    for hh in range(X_HEADS):
        lo, hi = hh * X_HEAD_DIM, (hh + 1) * X_HEAD_DIM
        s = _dot_nt(q[:, lo:hi], kv_ref[0, 0, :, lo:hi])
        p = jnp.exp(s - jnp.max(s, axis=-1, keepdims=True))
        l = jnp.sum(p, axis=-1, keepdims=True)
        o = _dot(p.astype(BF16), kv_ref[0, 0, :, D_MODEL + lo:D_MODEL + hi]) / l
        heads.append(o.astype(BF16))
    o_all = jnp.concatenate(heads, axis=-1)
    o_ref[0] = h1 + _dot(o_all, wo_ref[0])


def _mix_xattn(h, attn, conv, w_mo, g, w_xq, kv, w_xo, layer):
    b, s, d = h.shape
    tm = ROW_TILE
    return pl.pallas_call(
        _mix_xattn_kernel,
        grid=(b, s // tm),
        in_specs=[
            pl.BlockSpec((1, tm, d), lambda i, j: (i, j, 0)),
            pl.BlockSpec((1, tm, ATTN_WIDTH), lambda i, j: (i, j, 0)),
            pl.BlockSpec((1, tm, CONV_WIDTH_CH), lambda i, j: (i, j, 0)),
            _layer_resident((d, d), layer),
            _layer_resident((1, d), layer),
            _layer_resident((d, d), layer),
            pl.BlockSpec((1, 1, MEM_LEN, 2 * d), lambda i, j: (layer, i, 0, 0)),
            _layer_resident((d, d), layer),
        ],
        out_specs=pl.BlockSpec((1, tm, d), lambda i, j: (i, j, 0)),
        out_shape=jax.ShapeDtypeStruct((b, s, d), F32),
        compiler_params=pltpu.CompilerParams(
            dimension_semantics=("arbitrary", "arbitrary"), vmem_limit_bytes=VMEM_LIMIT),
        name="mix_xattn",
    )(h, attn, conv, w_mo, g, w_xq, kv, w_xo)


def _ffn_kernel(h_ref, g_ref, w1_ref, w2_ref, fg_ref, o_ref, acc_ref, *, final_norm):
    h = h_ref[...]
    xn = _rms(h, g_ref[0], NORM_EPS).astype(BF16)
    for c in range(D_FF // FF_CHUNK):
        lo, hi = c * FF_CHUNK, (c + 1) * FF_CHUNK
        f = jnp.square(jnp.maximum(_dot(xn, w1_ref[0, :, lo:hi]), 0.0)).astype(BF16)
        part = _dot(f, w2_ref[0, lo:hi, :])
        if c == 0:
            acc_ref[...] = part
        else:
            acc_ref[...] += part
    out = h + acc_ref[...]
    if final_norm:
        out = _rms(out, fg_ref[...], NORM_EPS)
    o_ref[...] = out


def _ffn(h2d, g, w1, w2, final_g, final_norm, layer):
    n, d = h2d.shape
    tm = ROW_TILE
    return pl.pallas_call(
        functools.partial(_ffn_kernel, final_norm=final_norm),
        grid=(n // tm,),
        in_specs=[
            pl.BlockSpec((tm, d), lambda i: (i, 0)),
            _layer_resident((1, d), layer),
            _layer_resident((d, D_FF), layer),
            _layer_resident((D_FF, d), layer),
            _resident((1, d)),
        ],
        out_specs=pl.BlockSpec((tm, d), lambda i: (i, 0)),
        out_shape=jax.ShapeDtypeStruct((n, d), F32),
        scratch_shapes=[pltpu.VMEM((tm, d), F32)],
        compiler_params=pltpu.CompilerParams(
            dimension_semantics=("arbitrary",), vmem_limit_bytes=VMEM_LIMIT),
        name="ffn",
    )(h2d, g, w1, w2, final_g)


def _mem_kv_kernel(m_ref, g_ref, w_ref, o_ref):
    mn = _rms(m_ref[...], g_ref[...], NORM_EPS).astype(BF16)
    o_ref[0] = _dot(mn, w_ref[0]).astype(BF16)


def _mem_kv(mem2d, g, w_xkv):
    n, d = mem2d.shape
    depth, _, n_cols = w_xkv.shape
    tm = ROW_TILE
    return pl.pallas_call(
        _mem_kv_kernel,
        grid=(depth, n // tm),
        in_specs=[
            pl.BlockSpec((tm, d), lambda l, i: (i, 0)),
            pl.BlockSpec((1, d), lambda l, i: (0, 0)),
            pl.BlockSpec((1, d, n_cols), lambda l, i: (l, 0, 0)),
        ],
        out_specs=pl.BlockSpec((1, tm, n_cols), lambda l, i: (l, i, 0)),
        out_shape=jax.ShapeDtypeStruct((depth, n, n_cols), BF16),
        compiler_params=pltpu.CompilerParams(
            dimension_semantics=("arbitrary", "arbitrary"), vmem_limit_bytes=VMEM_LIMIT),
        name="mem_kv",
    )(mem2d, g, w_xkv)


def kernel(x, mem, positions, norm_mix_g, w_in, lam_q1, lam_k1, lam_q2, lam_k2, subln_g, conv_w,
           w_mix_out, norm_x_g, mem_norm_g, w_xq, w_xkv, w_xo, norm_ffn_g, w_ff1, w_ff2, final_g):
    b, s, d = x.shape
    aw = ATTN_WIDTH
    pos3 = positions.reshape(b, 1, s)
    inv_freq = (ROPE_THETA ** (-jnp.arange(0, ROT_DIM, 2, dtype=F32) / ROT_DIM)).reshape(ROT_DIM // 2, 1)
    w_in_b = w_in.astype(BF16)
    w_vt_b = jnp.swapaxes(lax.optimization_barrier(w_in[:, :, 2 * aw:3 * aw]), 1, 2).astype(BF16)
    w_mo_b, w_xq_b, w_xo_b = w_mix_out.astype(BF16), w_xq.astype(BF16), w_xo.astype(BF16)
    w_ff1_b, w_ff2_b = w_ff1.astype(BF16), w_ff2.astype(BF16)
    g_mix, g_x, g_ffn = (g.reshape(DEPTH, 1, d) for g in (norm_mix_g, norm_x_g, norm_ffn_g))
    lam_rows = jnp.stack([lam_q1, lam_k1, lam_q2, lam_k2], axis=1).astype(F32)
    subln_col = subln_g.reshape(DEPTH, HEAD_W, 1)

    kv = _mem_kv(mem.reshape(b * MEM_LEN, d), mem_norm_g.reshape(1, d), w_xkv.astype(BF16))
    kv = kv.reshape(DEPTH, b, MEM_LEN, 2 * d)
    h = x
    for l in range(DEPTH):
        lambda_init = 0.8 - 0.6 * math.exp(-0.3 * l)
        q, k, vt, conv = _in_proj(h, pos3, inv_freq, g_mix, w_in_b, w_vt_b, conv_w, l)
        attn = _diff_attn(lam_rows, q, k, vt, subln_col, lambda_init, l)
        h = _mix_xattn(h, attn, conv, w_mo_b, g_x, w_xq_b, kv, w_xo_b, l)
        h = _ffn(h.reshape(b * s, d), g_ffn, w_ff1_b, w_ff2_b, final_g.reshape(1, d),
                 l == DEPTH - 1, l).reshape(b, s, d)
    return h
```

```python
import functools
import math

import jax
import jax.numpy as jnp
from jax import lax
from jax.experimental import pallas as pl
from jax.experimental.pallas import tpu as pltpu

D_MODEL = 1024
DEPTH = 2
MEM_LEN = 256
ATTN_WIDTH = D_MODEL // 2
CONV_WIDTH_CH = D_MODEL - ATTN_WIDTH
DIFF_HEAD_DIM = 64
DIFF_HEADS = ATTN_WIDTH // (2 * DIFF_HEAD_DIM)
HEAD_W = 2 * DIFF_HEAD_DIM
CONV_K = 3
ROT_DIM = DIFF_HEAD_DIM // 4
ROPE_THETA = 500000.0
X_HEADS = 4
X_HEAD_DIM = D_MODEL // X_HEADS
D_FF = 4 * D_MODEL
NEG_INF = -1e30
NORM_EPS = 1e-6
SUBLN_EPS = 1e-5

LANES = 128
SUBLANES = 8
VMEM_LIMIT = 56 * 1024 * 1024

ROW_TILE = 1024
Q_TILE = 512
ATTN_HEADS_PER_STEP = 2
SUM_ROWS = 16
FF_CHUNK = 1024

BF16 = jnp.bfloat16
F32 = jnp.float32


def _rms(x, g, eps):
    return x * lax.rsqrt(jnp.mean(x * x, axis=-1, keepdims=True) + eps) * g


def _dot(a, b):
    return jnp.dot(a, b, preferred_element_type=F32)


def _dot_nt(a, b):
    return lax.dot_general(a, b, (((1,), (1,)), ((), ())), preferred_element_type=F32)


def _resident(shape):
    zeros = (0,) * len(shape)
    return pl.BlockSpec(shape, lambda *_: zeros, pipeline_mode=pl.Buffered(1))


def _layer_resident(shape, layer):
    index = (layer,) + (0,) * len(shape)
    return pl.BlockSpec((1,) + tuple(shape), lambda *_: index, pipeline_mode=pl.Buffered(1))


def _in_proj_kernel(pos_ref, inv_freq_ref, x_ref, g_ref, w_ref, wvt_ref, cw_ref,
                    q_ref, k_ref, vt_ref, c_ref, carry_ref):
    tm = x_ref.shape[1]
    aw = ATTN_WIDTH

    @pl.when(pl.program_id(1) == 0)
    def _():
        carry_ref[...] = jnp.zeros_like(carry_ref)

    xn = _rms(x_ref[0], g_ref[0], NORM_EPS).astype(BF16)

    half = ROT_DIM // 2
    ang_t = inv_freq_ref[...] * pos_ref[0].astype(F32)
    cos_t, sin_t = jnp.cos(ang_t), jnp.sin(ang_t)
    rest = DIFF_HEAD_DIM - ROT_DIM
    ones, zeros = jnp.ones((rest, tm), F32), jnp.zeros((rest, tm), F32)
    cos = jnp.concatenate([cos_t, cos_t, ones] * 2, axis=0).T
    sin = jnp.concatenate([-sin_t, sin_t, zeros] * 2, axis=0).T
    lane = lax.broadcasted_iota(jnp.int32, (tm, HEAD_W), 1)
    pairs_up = (lane & (DIFF_HEAD_DIM - 1)) < half

    def rotate(t):
        up = pltpu.roll(t, LANES - half, axis=1)
        dn = pltpu.roll(t, half, axis=1)
        return t * cos + jnp.where(pairs_up, up, dn) * sin

    scale = math.log2(math.e) * DIFF_HEAD_DIM ** -0.5
    q = _dot(xn, w_ref[0, :, 0:aw])
    k = _dot(xn, w_ref[0, :, aw:2 * aw])
    for h in range(DIFF_HEADS):
        lo, hi = h * HEAD_W, (h + 1) * HEAD_W
        q_ref[0, :, lo:hi] = (rotate(q[:, lo:hi]) * scale).astype(BF16)
        k_ref[0, :, lo:hi] = rotate(k[:, lo:hi]).astype(BF16)

    cwid = CONV_WIDTH_CH
    c_gate = _dot(xn, w_ref[0, :, 3 * aw + cwid:3 * aw + 2 * cwid])
    hc = _dot(xn, w_ref[0, :, 3 * aw + 2 * cwid:3 * aw + 3 * cwid])
    b_gate = _dot(xn, w_ref[0, :, 3 * aw:3 * aw + cwid])
    u = c_gate * hc
    prev = carry_ref[...]
    row = lax.broadcasted_iota(jnp.int32, u.shape, 0)
    last = prev[SUBLANES - 1:SUBLANES, :]
    last2 = prev[SUBLANES - 2:SUBLANES - 1, :]
    u1 = jnp.where(row == 0, last, pltpu.roll(u, 1, axis=0))
    u2 = jnp.where(row == 0, last2, jnp.where(row == 1, last, pltpu.roll(u, 2, axis=0)))
    cw = cw_ref[0]
    y = cw[0:1, :] * u2 + cw[1:2, :] * u1 + cw[2:3, :] * u
    c_ref[0] = (b_gate * y).astype(BF16)
    carry_ref[...] = u[tm - SUBLANES:tm, :]

    vt = _dot_nt(wvt_ref[0], xn).astype(BF16)
    tk = vt_ref.shape[3]
    for i in range(tm // tk):
        vt_ref[0, i] = vt[:, i * tk:(i + 1) * tk]


def _in_proj(h, pos3, inv_freq, g, w_in, w_vt, conv_w, layer):
    b, s, d = h.shape
    tm = ROW_TILE
    n_cols = w_in.shape[2]
    act = jax.ShapeDtypeStruct((b, s, ATTN_WIDTH), BF16)
    act_spec = pl.BlockSpec((1, tm, ATTN_WIDTH), lambda i, j: (i, j, 0))
    tk = Q_TILE
    vt = jax.ShapeDtypeStruct((b, s // tk, ATTN_WIDTH, tk), BF16)
    vt_spec = pl.BlockSpec((1, tm // tk, ATTN_WIDTH, tk), lambda i, j: (i, j, 0, 0))
    return pl.pallas_call(
        _in_proj_kernel,
        grid=(b, s // tm),
        in_specs=[
            pl.BlockSpec((1, 1, tm), lambda i, j: (i, 0, j)),
            _resident((ROT_DIM // 2, 1)),
            pl.BlockSpec((1, tm, d), lambda i, j: (i, j, 0)),
            _layer_resident((1, d), layer),
            _layer_resident((d, n_cols), layer),
            _layer_resident((ATTN_WIDTH, d), layer),
            _layer_resident((CONV_K, CONV_WIDTH_CH), layer),
        ],
        out_specs=[act_spec, act_spec, vt_spec, act_spec],
        out_shape=[act, act, vt, act],
        scratch_shapes=[pltpu.VMEM((SUBLANES, CONV_WIDTH_CH), F32)],
        compiler_params=pltpu.CompilerParams(
            dimension_semantics=("arbitrary", "arbitrary"), vmem_limit_bytes=VMEM_LIMIT),
        name="in_proj",
    )(pos3, inv_freq, h, g, w_in, w_vt, conv_w)


def _diff_attn_kernel(lam_ref, q_ref, k_ref, vt_ref, g_ref, o_ref, *, out_scale, lambda_init):
    tq = Q_TILE
    tk = vt_ref.shape[3]
    n_q = q_ref.shape[1] // tq
    lam_rows = lam_ref[0]
    lam = (jnp.exp(jnp.sum(lam_rows[0:1, :] * lam_rows[1:2, :], axis=-1, keepdims=True))
           - jnp.exp(jnp.sum(lam_rows[2:3, :] * lam_rows[3:4, :], axis=-1, keepdims=True))
           + lambda_init)
    g_col = g_ref[0]
    ones_rows = jnp.ones((SUM_ROWS, tk), BF16)
    hq = tq // 2

    def pieces(qi, j):
        if (j + 1) * tk >= (qi + 1) * tq:
            return [(0, hq, hq, True), (hq, tq - hq, tk, True)]
        return [(0, tq, tk, False)]

    def scores(hd, qi, j):
        lanes = slice(hd * HEAD_W, (hd + 1) * HEAD_W)
        out = []
        for c0, nc, nk, _ in pieces(qi, j):
            qh = q_ref[0, qi * tq + c0:qi * tq + c0 + nc, lanes]
            first = lax.broadcasted_iota(jnp.int32, qh.shape, 1) < DIFF_HEAD_DIM
            zero = jnp.zeros_like(qh)
            kj = k_ref[0, j * tk:j * tk + nk, lanes]
            out.append((_dot_nt(kj, jnp.where(first, qh, zero)), _dot_nt(kj, jnp.where(first, zero, qh))))
        return out

    def finalize(hd, row0, state):
        (_, l0, a0), (_, l1, a1) = state
        ot = a0 / l0 - lam * (a1 / l1)
        ms = jnp.mean(ot * ot, axis=0, keepdims=True)
        ot = ot * lax.rsqrt(ms + SUBLN_EPS) * g_col * out_scale
        o_ref[0, row0:row0 + ot.shape[1], hd * HEAD_W:(hd + 1) * HEAD_W] = ot.T.astype(o_ref.dtype)

    def softmax_pv(hd, qi, j, s_cur, state):
        vt_aug = jnp.concatenate([vt_ref[0, j, hd * HEAD_W:(hd + 1) * HEAD_W, :], ones_rows], axis=0)
        new_state = []
        for (c0, nc, nk, masked), s_piece in zip(pieces(qi, j), s_cur):
            piece_state = []
            for c in range(2):
                s = s_piece[c]
                if masked:
                    key = lax.broadcasted_iota(jnp.int32, (nk, nc), 0)
                    qry = lax.broadcasted_iota(jnp.int32, (nk, nc), 1) + c0
                    s = jnp.where(key <= qry, s, NEG_INF)
                m_new = jnp.max(s, axis=0, keepdims=True)
                if j > 0:
                    m_old, l_old, acc_old = (a[:, c0:c0 + nc] for a in state[c])
                    m_new = jnp.maximum(m_old, m_new)
                    alpha = jnp.exp2(m_old - m_new)
                pv = _dot(vt_aug[:, 0:nk], jnp.exp2(s - m_new).astype(BF16))
                acc, l = pv[:HEAD_W], pv[HEAD_W:HEAD_W + 1]
                if j > 0:
                    acc, l = alpha * acc_old + acc, alpha * l_old + l
                piece_state.append((m_new, l, acc))
            if masked:
                finalize(hd, qi * tq + c0, piece_state)
            else:
                new_state = piece_state
        return new_state

    heads = range(q_ref.shape[2] // HEAD_W)
    pairs = [(qi, j) for qi in range(n_q) for j in range((qi * tq) // tk + 1)]
    s_next = [scores(hd, *pairs[0]) for hd in heads]
    state = [None for _ in heads]
    for t, (qi, j) in enumerate(pairs):
        s_cur = s_next
        if t + 1 < len(pairs):
            s_next = [scores(hd, *pairs[t + 1]) for hd in heads]
        state = [softmax_pv(hd, qi, j, s_cur[hd], state[hd]) for hd in heads]


def _diff_attn(lam_rows, q, k, vt, subln_g_col, lambda_init, layer):
    b, s, w = q.shape
    nk, tk = vt.shape[1], vt.shape[3]
    assert Q_TILE == tk, "diagonal tiles are assumed square"
    gw = ATTN_HEADS_PER_STEP * HEAD_W
    kern = functools.partial(_diff_attn_kernel, out_scale=1.0 - lambda_init, lambda_init=lambda_init)
    return pl.pallas_call(
        kern,
        grid=(b, DIFF_HEADS // ATTN_HEADS_PER_STEP),
        in_specs=[
            _layer_resident((4, DIFF_HEAD_DIM), layer),
            pl.BlockSpec((1, s, gw), lambda i, h: (i, 0, h)),
            pl.BlockSpec((1, s, gw), lambda i, h: (i, 0, h)),
            pl.BlockSpec((1, nk, gw, tk), lambda i, h: (i, 0, h, 0)),
            _layer_resident((HEAD_W, 1), layer),
        ],
        out_specs=pl.BlockSpec((1, s, gw), lambda i, h: (i, 0, h)),
        out_shape=jax.ShapeDtypeStruct((b, s, w), BF16),
        compiler_params=pltpu.CompilerParams(
            dimension_semantics=("arbitrary", "arbitrary"), vmem_limit_bytes=VMEM_LIMIT),
        name="diff_attn",
    )(lam_rows, q, k, vt, subln_g_col)


def _mix_xattn_kernel(h_ref, a_ref, c_ref, wmo_ref, g_ref, wq_ref, kv_ref, wo_ref, o_ref):
    aw = ATTN_WIDTH
    scale = X_HEAD_DIM ** -0.5
    h1 = h_ref[0] + _dot(a_ref[0], wmo_ref[0, 0:aw, :]) + _dot(c_ref[0], wmo_ref[0, aw:, :])
    xn = _rms(h1, g_ref[0], NORM_EPS).astype(BF16)
    q = (_dot(xn, wq_ref[0]) * scale).astype(BF16)
    heads = []
    for hh in range(X_HEADS):
        lo, hi = hh * X_HEAD_DIM, (hh + 1) * X_HEAD_DIM
        s = _dot_nt(q[:, lo:hi], kv_ref[0, 0, :, lo:hi])
        p = jnp.exp(s - jnp.max(s, axis=-1, keepdims=True))
        l = jnp.sum(p, axis=-1, keepdims=True)
        o = _dot(p.astype(BF16), kv_ref[0, 0, :, D_MODEL + lo:D_MODEL + hi]) / l
        heads.append(o.astype(BF16))
    o_all = jnp.concatenate(heads, axis=-1)
    o_ref[0] = h1 + _dot(o_all, wo_ref[0])


def _mix_xattn(h, attn, conv, w_mo, g, w_xq, kv, w_xo, layer):
    b, s, d = h.shape
    tm = ROW_TILE
    return pl.pallas_call(
        _mix_xattn_kernel,
        grid=(b, s // tm),
        in_specs=[
            pl.BlockSpec((1, tm, d), lambda i, j: (i, j, 0)),
            pl.BlockSpec((1, tm, ATTN_WIDTH), lambda i, j: (i, j, 0)),
            pl.BlockSpec((1, tm, CONV_WIDTH_CH), lambda i, j: (i, j, 0)),
            _layer_resident((d, d), layer),
            _layer_resident((1, d), layer),
            _layer_resident((d, d), layer),
            pl.BlockSpec((1, 1, MEM_LEN, 2 * d), lambda i, j: (layer, i, 0, 0)),
            _layer_resident((d, d), layer),
        ],
        out_specs=pl.BlockSpec((1, tm, d), lambda i, j: (i, j, 0)),
        out_shape=jax.ShapeDtypeStruct((b, s, d), F32),
        compiler_params=pltpu.CompilerParams(
            dimension_semantics=("arbitrary", "arbitrary"), vmem_limit_bytes=VMEM_LIMIT),
        name="mix_xattn",
    )(h, attn, conv, w_mo, g, w_xq, kv, w_xo)


def _ffn_kernel(h_ref, g_ref, w1_ref, w2_ref, fg_ref, o_ref, acc_ref, *, final_norm):
    h = h_ref[...]
    xn = _rms(h, g_ref[0], NORM_EPS).astype(BF16)
    for c in range(D_FF // FF_CHUNK):
        lo, hi = c * FF_CHUNK, (c + 1) * FF_CHUNK
        f = jnp.square(jnp.maximum(_dot(xn, w1_ref[0, :, lo:hi]), 0.0)).astype(BF16)
        part = _dot(f, w2_ref[0, lo:hi, :])
        if c == 0:
            acc_ref[...] = part
        else:
            acc_ref[...] += part
    out = h + acc_ref[...]
    if final_norm:
        out = _rms(out, fg_ref[...], NORM_EPS)
    o_ref[...] = out


def _ffn(h2d, g, w1, w2, final_g, final_norm, layer):
    n, d = h2d.shape
    tm = ROW_TILE
    return pl.pallas_call(
        functools.partial(_ffn_kernel, final_norm=final_norm),
        grid=(n // tm,),
        in_specs=[
            pl.BlockSpec((tm, d), lambda i: (i, 0)),
            _layer_resident((1, d), layer),
            _layer_resident((d, D_FF), layer),
            _layer_resident((D_FF, d), layer),
            _resident((1, d)),
        ],
        out_specs=pl.BlockSpec((tm, d), lambda i: (i, 0)),
        out_shape=jax.ShapeDtypeStruct((n, d), F32),
        scratch_shapes=[pltpu.VMEM((tm, d), F32)],
        compiler_params=pltpu.CompilerParams(
            dimension_semantics=("arbitrary",), vmem_limit_bytes=VMEM_LIMIT),
        name="ffn",
    )(h2d, g, w1, w2, final_g)


def _mem_kv_kernel(m_ref, g_ref, w_ref, o_ref):
    mn = _rms(m_ref[...], g_ref[...], NORM_EPS).astype(BF16)
    o_ref[0] = _dot(mn, w_ref[0]).astype(BF16)


def _mem_kv(mem2d, g, w_xkv):
    n, d = mem2d.shape
    depth, _, n_cols = w_xkv.shape
    tm = ROW_TILE
    return pl.pallas_call(
        _mem_kv_kernel,
        grid=(depth, n // tm),
        in_specs=[
            pl.BlockSpec((tm, d), lambda l, i: (i, 0)),
            pl.BlockSpec((1, d), lambda l, i: (0, 0)),
            pl.BlockSpec((1, d, n_cols), lambda l, i: (l, 0, 0)),
        ],
        out_specs=pl.BlockSpec((1, tm, n_cols), lambda l, i: (l, i, 0)),
        out_shape=jax.ShapeDtypeStruct((depth, n, n_cols), BF16),
        compiler_params=pltpu.CompilerParams(
            dimension_semantics=("arbitrary", "arbitrary"), vmem_limit_bytes=VMEM_LIMIT),
        name="mem_kv",
    )(mem2d, g, w_xkv)


def kernel(x, mem, positions, norm_mix_g, w_in, lam_q1, lam_k1, lam_q2, lam_k2, subln_g, conv_w,
           w_mix_out, norm_x_g, mem_norm_g, w_xq, w_xkv, w_xo, norm_ffn_g, w_ff1, w_ff2, final_g):
    b, s, d = x.shape
    aw = ATTN_WIDTH
    pos3 = positions.reshape(b, 1, s)
    inv_freq = (ROPE_THETA ** (-jnp.arange(0, ROT_DIM, 2, dtype=F32) / ROT_DIM)).reshape(ROT_DIM // 2, 1)
    w_in_b = w_in.astype(BF16)
    w_vt_b = jnp.swapaxes(lax.optimization_barrier(w_in[:, :, 2 * aw:3 * aw]), 1, 2).astype(BF16)
    w_mo_b, w_xq_b, w_xo_b = w_mix_out.astype(BF16), w_xq.astype(BF16), w_xo.astype(BF16)
    w_ff1_b, w_ff2_b = w_ff1.astype(BF16), w_ff2.astype(BF16)
    g_mix, g_x, g_ffn = (g.reshape(DEPTH, 1, d) for g in (norm_mix_g, norm_x_g, norm_ffn_g))
    lam_rows = jnp.stack([lam_q1, lam_k1, lam_q2, lam_k2], axis=1).astype(F32)
    subln_col = subln_g.reshape(DEPTH, HEAD_W, 1)

    kv = _mem_kv(mem.reshape(b * MEM_LEN, d), mem_norm_g.reshape(1, d), w_xkv.astype(BF16))
    kv = kv.reshape(DEPTH, b, MEM_LEN, 2 * d)
    h = x
    for l in range(DEPTH):
        lambda_init = 0.8 - 0.6 * math.exp(-0.3 * l)
        q, k, vt, conv = _in_proj(h, pos3, inv_freq, g_mix, w_in_b, w_vt_b, conv_w, l)
        attn = _diff_attn(lam_rows, q, k, vt, subln_col, lambda_init, l)
        h = _mix_xattn(h, attn, conv, w_mo_b, g_x, w_xq_b, kv, w_xo_b, l)
        h = _ffn(h.reshape(b * s, d), g_ffn, w_ff1_b, w_ff2_b, final_g.reshape(1, d),
                 l == DEPTH - 1, l).reshape(b, s, d)
    return h
```

```python
import functools
import math

import jax
import jax.numpy as jnp
from jax import lax
from jax.experimental import pallas as pl
from jax.experimental.pallas import tpu as pltpu

D_MODEL = 1024
DEPTH = 2
MEM_LEN = 256
ATTN_WIDTH = D_MODEL // 2
CONV_WIDTH_CH = D_MODEL - ATTN_WIDTH
DIFF_HEAD_DIM = 64
DIFF_HEADS = ATTN_WIDTH // (2 * DIFF_HEAD_DIM)
HEAD_W = 2 * DIFF_HEAD_DIM
CONV_K = 3
ROT_DIM = DIFF_HEAD_DIM // 4
ROPE_THETA = 500000.0
X_HEADS = 4
X_HEAD_DIM = D_MODEL // X_HEADS
D_FF = 4 * D_MODEL
NEG_INF = -1e30
NORM_EPS = 1e-6
SUBLN_EPS = 1e-5

LANES = 128
SUBLANES = 8
VMEM_LIMIT = 56 * 1024 * 1024

ROW_TILE = 1024
Q_TILE = 512
ATTN_HEADS_PER_STEP = 2
SUM_ROWS = 16
FF_CHUNK = 1024

BF16 = jnp.bfloat16
F32 = jnp.float32


def _rms(x, g, eps):
    return x * lax.rsqrt(jnp.mean(x * x, axis=-1, keepdims=True) + eps) * g


def _dot(a, b):
    return jnp.dot(a, b, preferred_element_type=F32)


def _dot_nt(a, b):
    return lax.dot_general(a, b, (((1,), (1,)), ((), ())), preferred_element_type=F32)


def _resident(shape):
    zeros = (0,) * len(shape)
    return pl.BlockSpec(shape, lambda *_: zeros, pipeline_mode=pl.Buffered(1))


def _layer_resident(shape, layer):
    index = (layer,) + (0,) * len(shape)
    return pl.BlockSpec((1,) + tuple(shape), lambda *_: index, pipeline_mode=pl.Buffered(1))


def _cast_plan(casts, n_steps, step_of):
    in_specs, out_specs, out_shapes = [], [], []
    for w, layer in casts:
        r, c = w.shape[1:]
        rows = r // n_steps
        in_specs.append(pl.BlockSpec((1, rows, c), lambda *g, layer=layer: (layer, step_of(*g), 0)))
        out_specs.append(pl.BlockSpec((1, rows, c), lambda *g: (0, step_of(*g), 0)))
        out_shapes.append(jax.ShapeDtypeStruct((1, r, c), BF16))
    return in_specs, out_specs, out_shapes


def _with_casts(body, n_in, n_out, n_cast):
    def kernel(*refs):
        ins, cast_in = refs[:n_in], refs[n_in:n_in + n_cast]
        outs = refs[n_in + n_cast:n_in + n_cast + n_out]
        cast_out = refs[n_in + n_cast + n_out:n_in + 2 * n_cast + n_out]
        scratch = refs[n_in + 2 * n_cast + n_out:]
        for src, dst in zip(cast_in, cast_out):
            dst[...] = src[...].astype(BF16)
        body(*ins, *outs, *scratch)
    return kernel


def _in_proj_kernel(pos_ref, inv_freq_ref, x_ref, g_ref, w_ref, wvt_ref, cw_ref,
                    q_ref, k_ref, vt_ref, c_ref, carry_ref):
    tm = x_ref.shape[1]
    aw = ATTN_WIDTH

    @pl.when(pl.program_id(1) == 0)
    def _():
        carry_ref[...] = jnp.zeros_like(carry_ref)

    xn = _rms(x_ref[0], g_ref[0], NORM_EPS).astype(BF16)

    half = ROT_DIM // 2
    ang_t = inv_freq_ref[...] * pos_ref[0].astype(F32)
    cos_t, sin_t = jnp.cos(ang_t), jnp.sin(ang_t)
    rest = DIFF_HEAD_DIM - ROT_DIM
    ones, zeros = jnp.ones((rest, tm), F32), jnp.zeros((rest, tm), F32)
    cos = jnp.concatenate([cos_t, cos_t, ones] * 2, axis=0).T
    sin = jnp.concatenate([-sin_t, sin_t, zeros] * 2, axis=0).T
    lane = lax.broadcasted_iota(jnp.int32, (tm, HEAD_W), 1)
    pairs_up = (lane & (DIFF_HEAD_DIM - 1)) < half

    def rotate(t):
        up = pltpu.roll(t, LANES - half, axis=1)
        dn = pltpu.roll(t, half, axis=1)
        return t * cos + jnp.where(pairs_up, up, dn) * sin

    scale = math.log2(math.e) * DIFF_HEAD_DIM ** -0.5
    q = _dot(xn, w_ref[0, :, 0:aw])
    k = _dot(xn, w_ref[0, :, aw:2 * aw])
    for h in range(DIFF_HEADS):
        lo, hi = h * HEAD_W, (h + 1) * HEAD_W
        q_ref[0, :, lo:hi] = (rotate(q[:, lo:hi]) * scale).astype(BF16)
        k_ref[0, :, lo:hi] = rotate(k[:, lo:hi]).astype(BF16)

    cwid = CONV_WIDTH_CH
    c_gate = _dot(xn, w_ref[0, :, 3 * aw + cwid:3 * aw + 2 * cwid])
    hc = _dot(xn, w_ref[0, :, 3 * aw + 2 * cwid:3 * aw + 3 * cwid])
    b_gate = _dot(xn, w_ref[0, :, 3 * aw:3 * aw + cwid])
    u = c_gate * hc
    prev = carry_ref[...]
    row = lax.broadcasted_iota(jnp.int32, u.shape, 0)
    last = prev[SUBLANES - 1:SUBLANES, :]
    last2 = prev[SUBLANES - 2:SUBLANES - 1, :]
    u1 = jnp.where(row == 0, last, pltpu.roll(u, 1, axis=0))
    u2 = jnp.where(row == 0, last2, jnp.where(row == 1, last, pltpu.roll(u, 2, axis=0)))
    cw = cw_ref[0]
    y = cw[0:1, :] * u2 + cw[1:2, :] * u1 + cw[2:3, :] * u
    c_ref[0] = (b_gate * y).astype(BF16)
    carry_ref[...] = u[tm - SUBLANES:tm, :]

    vt = _dot_nt(wvt_ref[0], xn).astype(BF16)
    tk = vt_ref.shape[3]
    for i in range(tm // tk):
        vt_ref[0, i] = vt[:, i * tk:(i + 1) * tk]


def _in_proj(h, pos3, inv_freq, g, w_in, w_in_layer, w_vt, conv_w, layer, casts):
    b, s, d = h.shape
    tm = ROW_TILE
    n_cols = w_in.shape[2]
    act = jax.ShapeDtypeStruct((b, s, ATTN_WIDTH), BF16)
    act_spec = pl.BlockSpec((1, tm, ATTN_WIDTH), lambda i, j: (i, j, 0))
    tk = Q_TILE
    vt = jax.ShapeDtypeStruct((b, s // tk, ATTN_WIDTH, tk), BF16)
    vt_spec = pl.BlockSpec((1, tm // tk, ATTN_WIDTH, tk), lambda i, j: (i, j, 0, 0))
    n_j = s // tm
    cast_in, cast_out, cast_shapes = _cast_plan(casts, b * n_j, lambda i, j: i * n_j + j)
    in_specs = [
        pl.BlockSpec((1, 1, tm), lambda i, j: (i, 0, j)),
        _resident((ROT_DIM // 2, 1)),
        pl.BlockSpec((1, tm, d), lambda i, j: (i, j, 0)),
        _layer_resident((1, d), layer),
        _layer_resident((d, n_cols), w_in_layer),
        _layer_resident((ATTN_WIDTH, d), layer),
        _layer_resident((CONV_K, CONV_WIDTH_CH), layer),
    ]
    return pl.pallas_call(
        _with_casts(_in_proj_kernel, len(in_specs), 4, len(casts)),
        grid=(b, n_j),
        in_specs=in_specs + cast_in,
        out_specs=[act_spec, act_spec, vt_spec, act_spec] + cast_out,
        out_shape=[act, act, vt, act] + cast_shapes,
        scratch_shapes=[pltpu.VMEM((SUBLANES, CONV_WIDTH_CH), F32)],
        compiler_params=pltpu.CompilerParams(
            dimension_semantics=("arbitrary", "arbitrary"), vmem_limit_bytes=VMEM_LIMIT),
        name="in_proj",
    )(pos3, inv_freq, h, g, w_in, w_vt, conv_w, *(w for w, _ in casts))


def _diff_attn_kernel(lam_ref, q_ref, k_ref, vt_ref, g_ref, o_ref, *, out_scale, lambda_init):
    tq = Q_TILE
    tk = vt_ref.shape[3]
    n_q = q_ref.shape[1] // tq
    heads = range(q_ref.shape[2] // HEAD_W)
    lam_rows = lam_ref[0]
    lam = (jnp.exp(jnp.sum(lam_rows[0:1, :] * lam_rows[1:2, :], axis=-1, keepdims=True))
           - jnp.exp(jnp.sum(lam_rows[2:3, :] * lam_rows[3:4, :], axis=-1, keepdims=True))
           + lambda_init)
    g_col = g_ref[0]
    ones_rows = jnp.ones((SUM_ROWS, tk), BF16)
    hq = tq // 2

    def pieces(qi, j):
        if (j + 1) * tk >= (qi + 1) * tq:
            return [(0, hq, hq, True), (hq, tq - hq, tk, True)]
        return [(0, tq, tk, False)]

    def scores(hd, qi, j):
        lanes = slice(hd * HEAD_W, (hd + 1) * HEAD_W)
        out = []
        for c0, nc, nk, _ in pieces(qi, j):
            qh = q_ref[0, qi * tq + c0:qi * tq + c0 + nc, lanes]
            first = lax.broadcasted_iota(jnp.int32, qh.shape, 1) < DIFF_HEAD_DIM
            zero = jnp.zeros_like(qh)
            kj = k_ref[0, j * tk:j * tk + nk, lanes]
            out.append((_dot_nt(kj, jnp.where(first, qh, zero)), _dot_nt(kj, jnp.where(first, zero, qh))))
        return out

    def finalize(hd, row0, state):
        (_, l0, a0), (_, l1, a1) = state
        ot = a0 / l0 - lam * (a1 / l1)
        ms = jnp.mean(ot * ot, axis=0, keepdims=True)
        ot = ot * lax.rsqrt(ms + SUBLN_EPS) * g_col * out_scale
        o_ref[0, row0:row0 + ot.shape[1], hd * HEAD_W:(hd + 1) * HEAD_W] = ot.T.astype(o_ref.dtype)

    def softmax_pv(hd, qi, j, s_cur, state):
        vt_aug = jnp.concatenate([vt_ref[0, j, hd * HEAD_W:(hd + 1) * HEAD_W, :], ones_rows], axis=0)
        new_state = []
        for (c0, nc, nk, masked), s_piece in zip(pieces(qi, j), s_cur):
            piece_state = []
            for c in range(2):
                s = s_piece[c]
                if masked:
                    key = lax.broadcasted_iota(jnp.int32, (nk, nc), 0)
                    qry = lax.broadcasted_iota(jnp.int32, (nk, nc), 1) + c0
                    s = jnp.where(key <= qry, s, NEG_INF)
                m_new = jnp.max(s, axis=0, keepdims=True)
                if j > 0:
                    m_old, l_old, acc_old = (a[:, c0:c0 + nc] for a in state[c])
                    m_new = jnp.maximum(m_old, m_new)
                    alpha = jnp.exp2(m_old - m_new)
                pv = _dot(vt_aug[:, 0:nk], jnp.exp2(s - m_new).astype(BF16))
                acc, l = pv[:HEAD_W], pv[HEAD_W:HEAD_W + 1]
                if j > 0:
                    acc, l = alpha * acc_old + acc, alpha * l_old + l
                piece_state.append((m_new, l, acc))
            if masked:
                finalize(hd, qi * tq + c0, piece_state)
            else:
                new_state = piece_state
        return new_state

    pairs = [(qi, j) for qi in range(n_q) for j in range((qi * tq) // tk + 1)]
    s_next = [scores(hd, *pairs[0]) for hd in heads]
    state = [None for _ in heads]
    for t, (qi, j) in enumerate(pairs):
        s_cur = s_next
        if t + 1 < len(pairs):
            s_next = [scores(hd, *pairs[t + 1]) for hd in heads]
        state = [softmax_pv(hd, qi, j, s_cur[hd], state[hd]) for hd in heads]


def _diff_attn(lam_rows, q, k, vt, subln_g_col, lambda_init, layer, casts):
    b, s, w = q.shape
    nk, tk = vt.shape[1], vt.shape[3]
    assert Q_TILE == tk, "diagonal tiles are assumed square"
    gw = ATTN_HEADS_PER_STEP * HEAD_W
    n_g = DIFF_HEADS // ATTN_HEADS_PER_STEP
    kern = functools.partial(_diff_attn_kernel, out_scale=1.0 - lambda_init, lambda_init=lambda_init)
    cast_in, cast_out, cast_shapes = _cast_plan(casts, b * n_g, lambda i, h: i * n_g + h)
    in_specs = [
        _layer_resident((4, DIFF_HEAD_DIM), layer),
        pl.BlockSpec((1, s, gw), lambda i, h: (i, 0, h)),
        pl.BlockSpec((1, s, gw), lambda i, h: (i, 0, h)),
        pl.BlockSpec((1, nk, gw, tk), lambda i, h: (i, 0, h, 0)),
        _layer_resident((HEAD_W, 1), layer),
    ]
    return pl.pallas_call(
        _with_casts(kern, len(in_specs), 1, len(casts)),
        grid=(b, n_g),
        in_specs=in_specs + cast_in,
        out_specs=[pl.BlockSpec((1, s, gw), lambda i, h: (i, 0, h))] + cast_out,
        out_shape=[jax.ShapeDtypeStruct((b, s, w), BF16)] + cast_shapes,
        compiler_params=pltpu.CompilerParams(
            dimension_semantics=("arbitrary", "arbitrary"), vmem_limit_bytes=VMEM_LIMIT),
        name="diff_attn",
    )(lam_rows, q, k, vt, subln_g_col, *(w for w, _ in casts))


def _mix_xattn_kernel(h_ref, a_ref, c_ref, wmo_ref, g_ref, wq_ref, kv_ref, wo_ref, o_ref):
    aw = ATTN_WIDTH
    scale = X_HEAD_DIM ** -0.5
    h1 = h_ref[0] + _dot(a_ref[0], wmo_ref[0, 0:aw, :]) + _dot(c_ref[0], wmo_ref[0, aw:, :])
    xn = _rms(h1, g_ref[0], NORM_EPS).astype(BF16)
    q = (_dot(xn, wq_ref[0]) * scale).astype(BF16)
    heads = []
    for hh in range(X_HEADS):
        lo, hi = hh * X_HEAD_DIM, (hh + 1) * X_HEAD_DIM
        s = _dot_nt(q[:, lo:hi], kv_ref[0, 0, :, lo:hi])
        p = jnp.exp(s - jnp.max(s, axis=-1, keepdims=True))
        l = jnp.sum(p, axis=-1, keepdims=True)
        o = _dot(p.astype(BF16), kv_ref[0, 0, :, D_MODEL + lo:D_MODEL + hi]) / l
        heads.append(o.astype(BF16))
    o_all = jnp.concatenate(heads, axis=-1)
    o_ref[0] = h1 + _dot(o_all, wo_ref[0])


def _mix_xattn(h, attn, conv, w_mo, g, w_xq, kv, w_xo, layer, w_layer, casts):
    b, s, d = h.shape
    tm = ROW_TILE
    n_j = s // tm
    cast_in, cast_out, cast_shapes = _cast_plan(casts, b * n_j, lambda i, j: i * n_j + j)
    in_specs = [
        pl.BlockSpec((1, tm, d), lambda i, j: (i, j, 0)),
        pl.BlockSpec((1, tm, ATTN_WIDTH), lambda i, j: (i, j, 0)),
        pl.BlockSpec((1, tm, CONV_WIDTH_CH), lambda i, j: (i, j, 0)),
        _layer_resident((d, d), w_layer),
        _layer_resident((1, d), layer),
        _layer_resident((d, d), w_layer),
        pl.BlockSpec((1, 1, MEM_LEN, 2 * d), lambda i, j: (layer, i, 0, 0)),
        _layer_resident((d, d), w_layer),
    ]
    return pl.pallas_call(
        _with_casts(_mix_xattn_kernel, len(in_specs), 1, len(casts)),
        grid=(b, n_j),
        in_specs=in_specs + cast_in,
        out_specs=[pl.BlockSpec((1, tm, d), lambda i, j: (i, j, 0))] + cast_out,
        out_shape=[jax.ShapeDtypeStruct((b, s, d), F32)] + cast_shapes,
        compiler_params=pltpu.CompilerParams(
            dimension_semantics=("arbitrary", "arbitrary"), vmem_limit_bytes=VMEM_LIMIT),
        name="mix_xattn",
    )(h, attn, conv, w_mo, g, w_xq, kv, w_xo, *(w for w, _ in casts))


def _ffn_kernel(h_ref, g_ref, w1_ref, w2_ref, fg_ref, o_ref, acc_ref, *, final_norm):
    h = h_ref[...]
    xn = _rms(h, g_ref[0], NORM_EPS).astype(BF16)
    for c in range(D_FF // FF_CHUNK):
        lo, hi = c * FF_CHUNK, (c + 1) * FF_CHUNK
        f = jnp.square(jnp.maximum(_dot(xn, w1_ref[0, :, lo:hi]), 0.0)).astype(BF16)
        part = _dot(f, w2_ref[0, lo:hi, :])
        if c == 0:
            acc_ref[...] = part
        else:
            acc_ref[...] += part
    out = h + acc_ref[...]
    if final_norm:
        out = _rms(out, fg_ref[...], NORM_EPS)
    o_ref[...] = out


def _ffn(h2d, g, w1, w2, final_g, final_norm, layer, w_layer):
    n, d = h2d.shape
    tm = ROW_TILE
    return pl.pallas_call(
        functools.partial(_ffn_kernel, final_norm=final_norm),
        grid=(n // tm,),
        in_specs=[
            pl.BlockSpec((tm, d), lambda i: (i, 0)),
            _layer_resident((1, d), layer),
            _layer_resident((d, D_FF), w_layer),
            _layer_resident((D_FF, d), w_layer),
            _resident((1, d)),
        ],
        out_specs=pl.BlockSpec((tm, d), lambda i: (i, 0)),
        out_shape=jax.ShapeDtypeStruct((n, d), F32),
        scratch_shapes=[pltpu.VMEM((tm, d), F32)],
        compiler_params=pltpu.CompilerParams(
            dimension_semantics=("arbitrary",), vmem_limit_bytes=VMEM_LIMIT),
        name="ffn",
    )(h2d, g, w1, w2, final_g)


def _mem_kv_kernel(m_ref, g_ref, w_ref, o_ref):
    mn = _rms(m_ref[...], g_ref[...], NORM_EPS).astype(BF16)
    o_ref[0] = _dot(mn, w_ref[0]).astype(BF16)


def _mem_kv(mem2d, g, w_xkv):
    n, d = mem2d.shape
    depth, _, n_cols = w_xkv.shape
    tm = ROW_TILE
    return pl.pallas_call(
        _mem_kv_kernel,
        grid=(depth, n // tm),
        in_specs=[
            pl.BlockSpec((tm, d), lambda l, i: (i, 0)),
            pl.BlockSpec((1, d), lambda l, i: (0, 0)),
            pl.BlockSpec((1, d, n_cols), lambda l, i: (l, 0, 0)),
        ],
        out_specs=pl.BlockSpec((1, tm, n_cols), lambda l, i: (l, i, 0)),
        out_shape=jax.ShapeDtypeStruct((depth, n, n_cols), BF16),
        compiler_params=pltpu.CompilerParams(
            dimension_semantics=("arbitrary", "arbitrary"), vmem_limit_bytes=VMEM_LIMIT),
        name="mem_kv",
    )(mem2d, g, w_xkv)


def kernel(x, mem, positions, norm_mix_g, w_in, lam_q1, lam_k1, lam_q2, lam_k2, subln_g, conv_w,
           w_mix_out, norm_x_g, mem_norm_g, w_xq, w_xkv, w_xo, norm_ffn_g, w_ff1, w_ff2, final_g):
    b, s, d = x.shape
    aw = ATTN_WIDTH
    pos3 = positions.reshape(b, 1, s)
    inv_freq = (ROPE_THETA ** (-jnp.arange(0, ROT_DIM, 2, dtype=F32) / ROT_DIM)).reshape(ROT_DIM // 2, 1)
    w_in_b = w_in[0:1].astype(BF16)
    w_vt_b = jnp.swapaxes(lax.optimization_barrier(w_in[:, :, 2 * aw:3 * aw]), 1, 2).astype(BF16)
    g_mix, g_x, g_ffn = (g.reshape(DEPTH, 1, d) for g in (norm_mix_g, norm_x_g, norm_ffn_g))
    lam_rows = jnp.stack([lam_q1, lam_k1, lam_q2, lam_k2], axis=1).astype(F32)
    subln_col = subln_g.reshape(DEPTH, HEAD_W, 1)

    kv = _mem_kv(mem.reshape(b * MEM_LEN, d), mem_norm_g.reshape(1, d), w_xkv.astype(BF16))
    kv = kv.reshape(DEPTH, b, MEM_LEN, 2 * d)
    h = x
    for l in range(DEPTH):
        lambda_init = 0.8 - 0.6 * math.exp(-0.3 * l)
        q, k, vt, conv, w_mo_b, w_xq_b, w_xo_b = _in_proj(
            h, pos3, inv_freq, g_mix, w_in_b, 0, w_vt_b, conv_w, l,
            casts=[(w_mix_out, l), (w_xq, l), (w_xo, l)])
        attn, w_ff1_b, w_ff2_b = _diff_attn(lam_rows, q, k, vt, subln_col, lambda_init, l,
                                            casts=[(w_ff1, l), (w_ff2, l)])
        next_w_in = [(w_in, l + 1)] if l + 1 < DEPTH else []
        h, *w_in_next = _mix_xattn(h, attn, conv, w_mo_b, g_x, w_xq_b, kv, w_xo_b, l, 0, casts=next_w_in)
        if w_in_next:
            w_in_b = w_in_next[0]
        h = _ffn(h.reshape(b * s, d), g_ffn, w_ff1_b, w_ff2_b, final_g.reshape(1, d),
                 l == DEPTH - 1, l, 0).reshape(b, s, d)
    return h
```

```python
import functools
import math

import jax
import jax.numpy as jnp
from jax import lax
from jax.experimental import pallas as pl
from jax.experimental.pallas import tpu as pltpu

D_MODEL = 1024
DEPTH = 2
MEM_LEN = 256
ATTN_WIDTH = D_MODEL // 2
CONV_WIDTH_CH = D_MODEL - ATTN_WIDTH
DIFF_HEAD_DIM = 64
DIFF_HEADS = ATTN_WIDTH // (2 * DIFF_HEAD_DIM)
HEAD_W = 2 * DIFF_HEAD_DIM
CONV_K = 3
ROT_DIM = DIFF_HEAD_DIM // 4
ROPE_THETA = 500000.0
X_HEADS = 4
X_HEAD_DIM = D_MODEL // X_HEADS
D_FF = 4 * D_MODEL
NEG_INF = -1e30
NORM_EPS = 1e-6
SUBLN_EPS = 1e-5

LANES = 128
SUBLANES = 8
VMEM_LIMIT = 56 * 1024 * 1024

ROW_TILE = 1024
Q_TILE = 512
ATTN_HEADS_PER_STEP = 2
SUM_ROWS = 16
FF_CHUNK = 1024

BF16 = jnp.bfloat16
F32 = jnp.float32


def _rms(x, g, eps):
    return x * lax.rsqrt(jnp.mean(x * x, axis=-1, keepdims=True) + eps) * g


def _dot(a, b):
    return jnp.dot(a, b, preferred_element_type=F32)


def _dot_nt(a, b):
    return lax.dot_general(a, b, (((1,), (1,)), ((), ())), preferred_element_type=F32)


def _resident(shape):
    zeros = (0,) * len(shape)
    return pl.BlockSpec(shape, lambda *_: zeros, pipeline_mode=pl.Buffered(1))


def _layer_resident(shape, layer):
    index = (layer,) + (0,) * len(shape)
    return pl.BlockSpec((1,) + tuple(shape), lambda *_: index, pipeline_mode=pl.Buffered(1))


def _cast_plan(casts, n_steps, step_of):
    in_specs, out_specs, out_shapes = [], [], []
    for w, layer in casts:
        r, c = w.shape[1:]
        rows = r // n_steps
        in_specs.append(pl.BlockSpec((1, rows, c), lambda *g, layer=layer: (layer, step_of(*g), 0)))
        out_specs.append(pl.BlockSpec((1, rows, c), lambda *g: (0, step_of(*g), 0)))
        out_shapes.append(jax.ShapeDtypeStruct((1, r, c), BF16))
    return in_specs, out_specs, out_shapes


def _with_casts(body, n_in, n_out, n_cast):
    def kernel(*refs):
        ins, cast_in = refs[:n_in], refs[n_in:n_in + n_cast]
        outs = refs[n_in + n_cast:n_in + n_cast + n_out]
        cast_out = refs[n_in + n_cast + n_out:n_in + 2 * n_cast + n_out]
        scratch = refs[n_in + 2 * n_cast + n_out:]
        for src, dst in zip(cast_in, cast_out):
            dst[...] = src[...].astype(BF16)
        body(*ins, *outs, *scratch)
    return kernel


def _in_proj_kernel(pos_ref, inv_freq_ref, x_ref, g_ref, w_ref, wvt_ref, cw_ref,
                    q_ref, k_ref, vt_ref, c_ref, carry_ref):
    tm = x_ref.shape[1]
    aw = ATTN_WIDTH

    @pl.when(pl.program_id(1) == 0)
    def _():
        carry_ref[...] = jnp.zeros_like(carry_ref)

    xn = _rms(x_ref[0], g_ref[0], NORM_EPS).astype(BF16)

    half = ROT_DIM // 2
    ang_t = inv_freq_ref[...] * pos_ref[0].astype(F32)
    cos_t, sin_t = jnp.cos(ang_t), jnp.sin(ang_t)
    rest = DIFF_HEAD_DIM - ROT_DIM
    ones, zeros = jnp.ones((rest, tm), F32), jnp.zeros((rest, tm), F32)
    cos = jnp.concatenate([cos_t, cos_t, ones] * 2, axis=0).T
    sin = jnp.concatenate([-sin_t, sin_t, zeros] * 2, axis=0).T
    lane = lax.broadcasted_iota(jnp.int32, (tm, HEAD_W), 1)
    pairs_up = (lane & (DIFF_HEAD_DIM - 1)) < half

    def rotate(t):
        up = pltpu.roll(t, LANES - half, axis=1)
        dn = pltpu.roll(t, half, axis=1)
        return t * cos + jnp.where(pairs_up, up, dn) * sin

    scale = math.log2(math.e) * DIFF_HEAD_DIM ** -0.5
    q = _dot(xn, w_ref[0, :, 0:aw])
    k = _dot(xn, w_ref[0, :, aw:2 * aw])
    for h in range(DIFF_HEADS):
        lo, hi = h * HEAD_W, (h + 1) * HEAD_W
        q_ref[0, :, lo:hi] = (rotate(q[:, lo:hi]) * scale).astype(BF16)
        k_ref[0, :, lo:hi] = rotate(k[:, lo:hi]).astype(BF16)

    cwid = CONV_WIDTH_CH
    c_gate = _dot(xn, w_ref[0, :, 3 * aw + cwid:3 * aw + 2 * cwid])
    hc = _dot(xn, w_ref[0, :, 3 * aw + 2 * cwid:3 * aw + 3 * cwid])
    b_gate = _dot(xn, w_ref[0, :, 3 * aw:3 * aw + cwid])
    u = c_gate * hc
    prev = carry_ref[...]
    row = lax.broadcasted_iota(jnp.int32, u.shape, 0)
    last = prev[SUBLANES - 1:SUBLANES, :]
    last2 = prev[SUBLANES - 2:SUBLANES - 1, :]
    u1 = jnp.where(row == 0, last, pltpu.roll(u, 1, axis=0))
    u2 = jnp.where(row == 0, last2, jnp.where(row == 1, last, pltpu.roll(u, 2, axis=0)))
    cw = cw_ref[0]
    y = cw[0:1, :] * u2 + cw[1:2, :] * u1 + cw[2:3, :] * u
    c_ref[0] = (b_gate * y).astype(BF16)
    carry_ref[...] = u[tm - SUBLANES:tm, :]

    vt = _dot_nt(wvt_ref[0], xn).astype(BF16)
    tk = vt_ref.shape[3]
    for i in range(tm // tk):
        vt_ref[0, i] = vt[:, i * tk:(i + 1) * tk]


def _in_proj(h, pos3, inv_freq, g, w_in, w_in_layer, w_vt, conv_w, layer, casts):
    b, s, d = h.shape
    tm = ROW_TILE
    n_cols = w_in.shape[2]
    act = jax.ShapeDtypeStruct((b, s, ATTN_WIDTH), BF16)
    act_spec = pl.BlockSpec((1, tm, ATTN_WIDTH), lambda i, j: (i, j, 0))
    tk = Q_TILE
    vt = jax.ShapeDtypeStruct((b, s // tk, ATTN_WIDTH, tk), BF16)
    vt_spec = pl.BlockSpec((1, tm // tk, ATTN_WIDTH, tk), lambda i, j: (i, j, 0, 0))
    n_j = s // tm
    cast_in, cast_out, cast_shapes = _cast_plan(casts, b * n_j, lambda i, j: i * n_j + j)
    in_specs = [
        pl.BlockSpec((1, 1, tm), lambda i, j: (i, 0, j)),
        _resident((ROT_DIM // 2, 1)),
        pl.BlockSpec((1, tm, d), lambda i, j: (i, j, 0)),
        _layer_resident((1, d), layer),
        _layer_resident((d, n_cols), w_in_layer),
        _layer_resident((ATTN_WIDTH, d), layer),
        _layer_resident((CONV_K, CONV_WIDTH_CH), layer),
    ]
    return pl.pallas_call(
        _with_casts(_in_proj_kernel, len(in_specs), 4, len(casts)),
        grid=(b, n_j),
        in_specs=in_specs + cast_in,
        out_specs=[act_spec, act_spec, vt_spec, act_spec] + cast_out,
        out_shape=[act, act, vt, act] + cast_shapes,
        scratch_shapes=[pltpu.VMEM((SUBLANES, CONV_WIDTH_CH), F32)],
        compiler_params=pltpu.CompilerParams(
            dimension_semantics=("arbitrary", "arbitrary"), vmem_limit_bytes=VMEM_LIMIT),
        name="in_proj",
    )(pos3, inv_freq, h, g, w_in, w_vt, conv_w, *(w for w, _ in casts))


def _diff_attn_kernel(lam_ref, q_ref, k_ref, vt_ref, g_ref, o_ref, *, out_scale, lambda_init):
    tq = Q_TILE
    tk = vt_ref.shape[3]
    n_q = q_ref.shape[1] // tq
    heads = range(q_ref.shape[2] // HEAD_W)
    lam_rows = lam_ref[0]
    lam = (jnp.exp(jnp.sum(lam_rows[0:1, :] * lam_rows[1:2, :], axis=-1, keepdims=True))
           - jnp.exp(jnp.sum(lam_rows[2:3, :] * lam_rows[3:4, :], axis=-1, keepdims=True))
           + lambda_init)
    g_col = g_ref[0]
    ones_rows = jnp.ones((SUM_ROWS, tk), BF16)
    hq = tq // 2

    def pieces(qi, j):
        if (j + 1) * tk >= (qi + 1) * tq:
            return [(0, hq, hq, True), (hq, tq - hq, tk, True)]
        return [(0, tq, tk, False)]

    def scores(hd, qi, j):
        lanes = slice(hd * HEAD_W, (hd + 1) * HEAD_W)
        out = []
        for c0, nc, nk, _ in pieces(qi, j):
            qh = q_ref[0, qi * tq + c0:qi * tq + c0 + nc, lanes]
            first = lax.broadcasted_iota(jnp.int32, qh.shape, 1) < DIFF_HEAD_DIM
            zero = jnp.zeros_like(qh)
            kj = k_ref[0, j * tk:j * tk + nk, lanes]
            out.append((_dot_nt(kj, jnp.where(first, qh, zero)), _dot_nt(kj, jnp.where(first, zero, qh))))
        return out

    def finalize(hd, row0, state):
        (_, l0, a0), (_, l1, a1) = state
        ot = a0 / l0 - lam * (a1 / l1)
        ms = jnp.mean(ot * ot, axis=0, keepdims=True)
        ot = ot * lax.rsqrt(ms + SUBLN_EPS) * g_col * out_scale
        o_ref[0, row0:row0 + ot.shape[1], hd * HEAD_W:(hd + 1) * HEAD_W] = ot.T.astype(o_ref.dtype)

    def softmax_pv(hd, qi, j, s_cur, state):
        vt_aug = jnp.concatenate([vt_ref[0, j, hd * HEAD_W:(hd + 1) * HEAD_W, :], ones_rows], axis=0)
        new_state = []
        for (c0, nc, nk, masked), s_piece in zip(pieces(qi, j), s_cur):
            piece_state = []
            for c in range(2):
                s = s_piece[c]
                if masked:
                    key = lax.broadcasted_iota(jnp.int32, (nk, nc), 0)
                    qry = lax.broadcasted_iota(jnp.int32, (nk, nc), 1) + c0
                    s = jnp.where(key <= qry, s, NEG_INF)
                m_new = jnp.max(s, axis=0, keepdims=True)
                if j > 0:
                    m_old, l_old, acc_old = (a[:, c0:c0 + nc] for a in state[c])
                    m_new = jnp.maximum(m_old, m_new)
                    alpha = jnp.exp2(m_old - m_new)
                pv = _dot(vt_aug[:, 0:nk], jnp.exp2(s - m_new).astype(BF16))
                acc, l = pv[:HEAD_W], pv[HEAD_W:HEAD_W + 1]
                if j > 0:
                    acc, l = alpha * acc_old + acc, alpha * l_old + l
                piece_state.append((m_new, l, acc))
            if masked:
                finalize(hd, qi * tq + c0, piece_state)
            else:
                new_state = piece_state
        return new_state

    pairs = [(qi, j) for qi in range(n_q) for j in range((qi * tq) // tk + 1)]
    s_next = [scores(hd, *pairs[0]) for hd in heads]
    state = [None for _ in heads]
    for t, (qi, j) in enumerate(pairs):
        s_cur = s_next
        if t + 1 < len(pairs):
            s_next = [scores(hd, *pairs[t + 1]) for hd in heads]
        state = [softmax_pv(hd, qi, j, s_cur[hd], state[hd]) for hd in heads]


def _diff_attn(lam_rows, q, k, vt, subln_g_col, lambda_init, layer):
    b, s, w = q.shape
    nk, tk = vt.shape[1], vt.shape[3]
    assert Q_TILE == tk, "diagonal tiles are assumed square"
    gw = ATTN_HEADS_PER_STEP * HEAD_W
    kern = functools.partial(_diff_attn_kernel, out_scale=1.0 - lambda_init, lambda_init=lambda_init)
    return pl.pallas_call(
        kern,
        grid=(b, DIFF_HEADS // ATTN_HEADS_PER_STEP),
        in_specs=[
            _layer_resident((4, DIFF_HEAD_DIM), layer),
            pl.BlockSpec((1, s, gw), lambda i, h: (i, 0, h)),
            pl.BlockSpec((1, s, gw), lambda i, h: (i, 0, h)),
            pl.BlockSpec((1, nk, gw, tk), lambda i, h: (i, 0, h, 0)),
            _layer_resident((HEAD_W, 1), layer),
        ],
        out_specs=pl.BlockSpec((1, s, gw), lambda i, h: (i, 0, h)),
        out_shape=jax.ShapeDtypeStruct((b, s, w), BF16),
        compiler_params=pltpu.CompilerParams(
            dimension_semantics=("arbitrary", "arbitrary"), vmem_limit_bytes=VMEM_LIMIT),
        name="diff_attn",
    )(lam_rows, q, k, vt, subln_g_col)


def _mix_xattn_kernel(h_ref, a_ref, c_ref, wmo_ref, g_ref, wq_ref, kv_ref, wo_ref, o_ref):
    aw = ATTN_WIDTH
    scale = X_HEAD_DIM ** -0.5
    h1 = h_ref[0] + _dot(a_ref[0], wmo_ref[0, 0:aw, :]) + _dot(c_ref[0], wmo_ref[0, aw:, :])
    xn = _rms(h1, g_ref[0], NORM_EPS).astype(BF16)
    q = (_dot(xn, wq_ref[0]) * scale).astype(BF16)
    heads = []
    for hh in range(X_HEADS):
        lo, hi = hh * X_HEAD_DIM, (hh + 1) * X_HEAD_DIM
        s = _dot_nt(q[:, lo:hi], kv_ref[0, 0, :, lo:hi])
        p = jnp.exp(s - jnp.max(s, axis=-1, keepdims=True))
        l = jnp.sum(p, axis=-1, keepdims=True)
        o = _dot(p.astype(BF16), kv_ref[0, 0, :, D_MODEL + lo:D_MODEL + hi]) / l
        heads.append(o.astype(BF16))
    o_all = jnp.concatenate(heads, axis=-1)
    o_ref[0] = h1 + _dot(o_all, wo_ref[0])


def _mix_xattn(h, attn, conv, w_mo, g, w_xq, kv, w_xo, layer, w_layer, casts):
    b, s, d = h.shape
    tm = ROW_TILE
    n_j = s // tm
    cast_in, cast_out, cast_shapes = _cast_plan(casts, b * n_j, lambda i, j: i * n_j + j)
    in_specs = [
        pl.BlockSpec((1, tm, d), lambda i, j: (i, j, 0)),
        pl.BlockSpec((1, tm, ATTN_WIDTH), lambda i, j: (i, j, 0)),
        pl.BlockSpec((1, tm, CONV_WIDTH_CH), lambda i, j: (i, j, 0)),
        _layer_resident((d, d), w_layer),
        _layer_resident((1, d), layer),
        _layer_resident((d, d), w_layer),
        pl.BlockSpec((1, 1, MEM_LEN, 2 * d), lambda i, j: (layer, i, 0, 0)),
        _layer_resident((d, d), w_layer),
    ]
    return pl.pallas_call(
        _with_casts(_mix_xattn_kernel, len(in_specs), 1, len(casts)),
        grid=(b, n_j),
        in_specs=in_specs + cast_in,
        out_specs=[pl.BlockSpec((1, tm, d), lambda i, j: (i, j, 0))] + cast_out,
        out_shape=[jax.ShapeDtypeStruct((b, s, d), F32)] + cast_shapes,
        compiler_params=pltpu.CompilerParams(
            dimension_semantics=("arbitrary", "arbitrary"), vmem_limit_bytes=VMEM_LIMIT),
        name="mix_xattn",
    )(h, attn, conv, w_mo, g, w_xq, kv, w_xo, *(w for w, _ in casts))


def _ffn_kernel(h_ref, g_ref, w1_ref, w2_ref, fg_ref, o_ref, acc_ref, *, final_norm):
    h = h_ref[...]
    xn = _rms(h, g_ref[0], NORM_EPS).astype(BF16)
    for c in range(D_FF // FF_CHUNK):
        lo, hi = c * FF_CHUNK, (c + 1) * FF_CHUNK
        f = jnp.square(jnp.maximum(_dot(xn, w1_ref[0, :, lo:hi]), 0.0)).astype(BF16)
        part = _dot(f, w2_ref[0, lo:hi, :])
        if c == 0:
            acc_ref[...] = part
        else:
            acc_ref[...] += part
    out = h + acc_ref[...]
    if final_norm:
        out = _rms(out, fg_ref[...], NORM_EPS)
    o_ref[...] = out


def _ffn(h2d, g, w1, w2, final_g, final_norm, layer, w_layer):
    n, d = h2d.shape
    tm = ROW_TILE
    return pl.pallas_call(
        functools.partial(_ffn_kernel, final_norm=final_norm),
        grid=(n // tm,),
        in_specs=[
            pl.BlockSpec((tm, d), lambda i: (i, 0)),
            _layer_resident((1, d), layer),
            _layer_resident((d, D_FF), w_layer),
            _layer_resident((D_FF, d), w_layer),
            _resident((1, d)),
        ],
        out_specs=pl.BlockSpec((tm, d), lambda i: (i, 0)),
        out_shape=jax.ShapeDtypeStruct((n, d), F32),
        scratch_shapes=[pltpu.VMEM((tm, d), F32)],
        compiler_params=pltpu.CompilerParams(
            dimension_semantics=("arbitrary",), vmem_limit_bytes=VMEM_LIMIT),
        name="ffn",
    )(h2d, g, w1, w2, final_g)


def _mem_kv_kernel(m_ref, g_ref, w_ref, o_ref):
    mn = _rms(m_ref[...], g_ref[...], NORM_EPS).astype(BF16)
    o_ref[0] = _dot(mn, w_ref[0]).astype(BF16)


def _mem_kv(mem2d, g, w_xkv, casts):
    n, d = mem2d.shape
    depth, _, n_cols = w_xkv.shape
    tm = ROW_TILE
    n_i = n // tm
    cast_in, cast_out, cast_shapes = _cast_plan(casts, depth * n_i, lambda l, i: l * n_i + i)
    in_specs = [
        pl.BlockSpec((tm, d), lambda l, i: (i, 0)),
        pl.BlockSpec((1, d), lambda l, i: (0, 0)),
        pl.BlockSpec((1, d, n_cols), lambda l, i: (l, 0, 0)),
    ]
    return pl.pallas_call(
        _with_casts(_mem_kv_kernel, len(in_specs), 1, len(casts)),
        grid=(depth, n_i),
        in_specs=in_specs + cast_in,
        out_specs=[pl.BlockSpec((1, tm, n_cols), lambda l, i: (l, i, 0))] + cast_out,
        out_shape=[jax.ShapeDtypeStruct((depth, n, n_cols), BF16)] + cast_shapes,
        compiler_params=pltpu.CompilerParams(
            dimension_semantics=("arbitrary", "arbitrary"), vmem_limit_bytes=VMEM_LIMIT),
        name="mem_kv",
    )(mem2d, g, w_xkv, *(w for w, _ in casts))


def kernel(x, mem, positions, norm_mix_g, w_in, lam_q1, lam_k1, lam_q2, lam_k2, subln_g, conv_w,
           w_mix_out, norm_x_g, mem_norm_g, w_xq, w_xkv, w_xo, norm_ffn_g, w_ff1, w_ff2, final_g):
    b, s, d = x.shape
    aw = ATTN_WIDTH
    pos3 = positions.reshape(b, 1, s)
    inv_freq = (ROPE_THETA ** (-jnp.arange(0, ROT_DIM, 2, dtype=F32) / ROT_DIM)).reshape(ROT_DIM // 2, 1)
    w_vt_b = jnp.swapaxes(lax.optimization_barrier(w_in[:, :, 2 * aw:3 * aw]), 1, 2).astype(BF16)
    g_mix, g_x, g_ffn = (g.reshape(DEPTH, 1, d) for g in (norm_mix_g, norm_x_g, norm_ffn_g))
    lam_rows = jnp.stack([lam_q1, lam_k1, lam_q2, lam_k2], axis=1).astype(F32)
    subln_col = subln_g.reshape(DEPTH, HEAD_W, 1)

    kv, w_in_b = _mem_kv(mem.reshape(b * MEM_LEN, d), mem_norm_g.reshape(1, d), w_xkv.astype(BF16),
                         casts=[(w_in, 0)])
    kv = kv.reshape(DEPTH, b, MEM_LEN, 2 * d)
    h = x
    for l in range(DEPTH):
        lambda_init = 0.8 - 0.6 * math.exp(-0.3 * l)
        q, k, vt, conv, w_mo_b, w_xq_b, w_xo_b = _in_proj(
            h, pos3, inv_freq, g_mix, w_in_b, 0, w_vt_b, conv_w, l,
            casts=[(w_mix_out, l), (w_xq, l), (w_xo, l)])
        attn = _diff_attn(lam_rows, q, k, vt, subln_col, lambda_init, l)
        next_w_in = [(w_in, l + 1)] if l + 1 < DEPTH else []
        h, w_ff1_b, w_ff2_b, *w_in_next = _mix_xattn(
            h, attn, conv, w_mo_b, g_x, w_xq_b, kv, w_xo_b, l, 0,
            casts=[(w_ff1, l), (w_ff2, l)] + next_w_in)
        if w_in_next:
            w_in_b = w_in_next[0]
        h = _ffn(h.reshape(b * s, d), g_ffn, w_ff1_b, w_ff2_b, final_g.reshape(1, d),
                 l == DEPTH - 1, l, 0).reshape(b, s, d)
    return h
```

```python
import functools
import math

import jax
import jax.numpy as jnp
from jax import lax
from jax.experimental import pallas as pl
from jax.experimental.pallas import tpu as pltpu

D_MODEL = 1024
DEPTH = 2
MEM_LEN = 256
ATTN_WIDTH = D_MODEL // 2
CONV_WIDTH_CH = D_MODEL - ATTN_WIDTH
DIFF_HEAD_DIM = 64
DIFF_HEADS = ATTN_WIDTH // (2 * DIFF_HEAD_DIM)
HEAD_W = 2 * DIFF_HEAD_DIM
CONV_K = 3
ROT_DIM = DIFF_HEAD_DIM // 4
ROPE_THETA = 500000.0
X_HEADS = 4
X_HEAD_DIM = D_MODEL // X_HEADS
D_FF = 4 * D_MODEL
NEG_INF = -1e30
NORM_EPS = 1e-6
SUBLN_EPS = 1e-5

LANES = 128
SUBLANES = 8
VMEM_LIMIT = 56 * 1024 * 1024

ROW_TILE = 1024
Q_TILE = 512
ATTN_HEADS_PER_STEP = 4
SUM_ROWS = 16
FF_CHUNK = 1024

BF16 = jnp.bfloat16
F32 = jnp.float32


def _rms(x, g, eps):
    return x * lax.rsqrt(jnp.mean(x * x, axis=-1, keepdims=True) + eps) * g


def _dot(a, b):
    return jnp.dot(a, b, preferred_element_type=F32)


def _dot_nt(a, b):
    return lax.dot_general(a, b, (((1,), (1,)), ((), ())), preferred_element_type=F32)


def _resident(shape):
    zeros = (0,) * len(shape)
    return pl.BlockSpec(shape, lambda *_: zeros, pipeline_mode=pl.Buffered(1))


def _layer_resident(shape, layer):
    index = (layer,) + (0,) * len(shape)
    return pl.BlockSpec((1,) + tuple(shape), lambda *_: index, pipeline_mode=pl.Buffered(1))


def _cast_plan(casts, n_steps, step_of):
    in_specs, out_specs, out_shapes = [], [], []
    for w, layer in casts:
        r, c = w.shape[1:]
        assert r % n_steps == 0, (r, n_steps)
        rows = r // n_steps
        in_specs.append(pl.BlockSpec((1, rows, c), lambda *g, layer=layer: (layer, step_of(*g), 0)))
        out_specs.append(pl.BlockSpec((1, rows, c), lambda *g: (0, step_of(*g), 0)))
        out_shapes.append(jax.ShapeDtypeStruct((1, r, c), BF16))
    return in_specs, out_specs, out_shapes


def _with_casts(body, n_in, n_out, n_cast):
    def kernel(*refs):
        ins, cast_in = refs[:n_in], refs[n_in:n_in + n_cast]
        outs = refs[n_in + n_cast:n_in + n_cast + n_out]
        cast_out = refs[n_in + n_cast + n_out:n_in + 2 * n_cast + n_out]
        scratch = refs[n_in + 2 * n_cast + n_out:]
        for src, dst in zip(cast_in, cast_out):
            dst[...] = src[...].astype(BF16)
        body(*ins, *outs, *scratch)
    return kernel


def _in_proj_kernel(pos_ref, inv_freq_ref, x_ref, g_ref, w_ref, wvt_ref, cw_ref,
                    q_ref, k_ref, vt_ref, c_ref, carry_ref):
    tm = x_ref.shape[1]
    aw = ATTN_WIDTH

    @pl.when(pl.program_id(1) == 0)
    def _():
        carry_ref[...] = jnp.zeros_like(carry_ref)

    xn = _rms(x_ref[0], g_ref[0], NORM_EPS).astype(BF16)

    half = ROT_DIM // 2
    ang_t = inv_freq_ref[...] * pos_ref[0].astype(F32)
    cos_t, sin_t = jnp.cos(ang_t), jnp.sin(ang_t)
    rest = DIFF_HEAD_DIM - ROT_DIM
    ones, zeros = jnp.ones((rest, tm), F32), jnp.zeros((rest, tm), F32)
    cos = jnp.concatenate([cos_t, cos_t, ones] * 2, axis=0).T
    sin = jnp.concatenate([-sin_t, sin_t, zeros] * 2, axis=0).T
    lane = lax.broadcasted_iota(jnp.int32, (tm, HEAD_W), 1)
    pairs_up = (lane & (DIFF_HEAD_DIM - 1)) < half

    def rotate(t):
        up = pltpu.roll(t, LANES - half, axis=1)
        dn = pltpu.roll(t, half, axis=1)
        return t * cos + jnp.where(pairs_up, up, dn) * sin

    scale = math.log2(math.e) * DIFF_HEAD_DIM ** -0.5
    q = _dot(xn, w_ref[0, :, 0:aw])
    k = _dot(xn, w_ref[0, :, aw:2 * aw])
    for h in range(DIFF_HEADS):
        lo, hi = h * HEAD_W, (h + 1) * HEAD_W
        q_ref[0, :, lo:hi] = (rotate(q[:, lo:hi]) * scale).astype(BF16)
        k_ref[0, :, lo:hi] = rotate(k[:, lo:hi]).astype(BF16)

    cwid = CONV_WIDTH_CH
    c_gate = _dot(xn, w_ref[0, :, 3 * aw + cwid:3 * aw + 2 * cwid])
    hc = _dot(xn, w_ref[0, :, 3 * aw + 2 * cwid:3 * aw + 3 * cwid])
    b_gate = _dot(xn, w_ref[0, :, 3 * aw:3 * aw + cwid])
    u = c_gate * hc
    prev = carry_ref[...]
    row = lax.broadcasted_iota(jnp.int32, u.shape, 0)
    last = prev[SUBLANES - 1:SUBLANES, :]
    last2 = prev[SUBLANES - 2:SUBLANES - 1, :]
    u1 = jnp.where(row == 0, last, pltpu.roll(u, 1, axis=0))
    u2 = jnp.where(row == 0, last2, jnp.where(row == 1, last, pltpu.roll(u, 2, axis=0)))
    cw = cw_ref[0]
    y = cw[0:1, :] * u2 + cw[1:2, :] * u1 + cw[2:3, :] * u
    c_ref[0] = (b_gate * y).astype(BF16)
    carry_ref[...] = u[tm - SUBLANES:tm, :]

    vt = _dot_nt(wvt_ref[0], xn).astype(BF16)
    tk = vt_ref.shape[3]
    for i in range(tm // tk):
        vt_ref[0, i] = vt[:, i * tk:(i + 1) * tk]


def _in_proj(h, pos3, inv_freq, g, w_in, w_in_layer, w_vt, conv_w, layer, casts):
    b, s, d = h.shape
    tm = ROW_TILE
    n_cols = w_in.shape[2]
    act = jax.ShapeDtypeStruct((b, s, ATTN_WIDTH), BF16)
    act_spec = pl.BlockSpec((1, tm, ATTN_WIDTH), lambda i, j: (i, j, 0))
    tk = Q_TILE
    vt = jax.ShapeDtypeStruct((b, s // tk, ATTN_WIDTH, tk), BF16)
    vt_spec = pl.BlockSpec((1, tm // tk, ATTN_WIDTH, tk), lambda i, j: (i, j, 0, 0))
    n_j = s // tm
    cast_in, cast_out, cast_shapes = _cast_plan(casts, b * n_j, lambda i, j: i * n_j + j)
    in_specs = [
        pl.BlockSpec((1, 1, tm), lambda i, j: (i, 0, j)),
        _resident((ROT_DIM // 2, 1)),
        pl.BlockSpec((1, tm, d), lambda i, j: (i, j, 0)),
        _layer_resident((1, d), layer),
        _layer_resident((d, n_cols), w_in_layer),
        _layer_resident((ATTN_WIDTH, d), layer),
        _layer_resident((CONV_K, CONV_WIDTH_CH), layer),
    ]
    return pl.pallas_call(
        _with_casts(_in_proj_kernel, len(in_specs), 4, len(casts)),
        grid=(b, n_j),
        in_specs=in_specs + cast_in,
        out_specs=[act_spec, act_spec, vt_spec, act_spec] + cast_out,
        out_shape=[act, act, vt, act] + cast_shapes,
        scratch_shapes=[pltpu.VMEM((SUBLANES, CONV_WIDTH_CH), F32)],
        compiler_params=pltpu.CompilerParams(
            dimension_semantics=("arbitrary", "arbitrary"), vmem_limit_bytes=VMEM_LIMIT),
        name="in_proj",
    )(pos3, inv_freq, h, g, w_in, w_vt, conv_w, *(w for w, _ in casts))


def _diff_attn_kernel(lam_ref, q_ref, k_ref, vt_ref, g_ref, o_ref, *, out_scale, lambda_init):
    tq = Q_TILE
    tk = vt_ref.shape[3]
    n_q = q_ref.shape[1] // tq
    heads = range(q_ref.shape[2] // HEAD_W)
    lam_rows = lam_ref[0]
    lam = (jnp.exp(jnp.sum(lam_rows[0:1, :] * lam_rows[1:2, :], axis=-1, keepdims=True))
           - jnp.exp(jnp.sum(lam_rows[2:3, :] * lam_rows[3:4, :], axis=-1, keepdims=True))
           + lambda_init)
    g_col = g_ref[0]
    ones_rows = jnp.ones((SUM_ROWS, tk), BF16)
    hq = tq // 2

    def pieces(qi, j):
        if (j + 1) * tk >= (qi + 1) * tq:
            return [(0, hq, hq, True), (hq, tq - hq, tk, True)]
        return [(0, tq, tk, False)]

    def scores(hd, qi, j):
        lanes = slice(hd * HEAD_W, (hd + 1) * HEAD_W)
        out = []
        for c0, nc, nk, _ in pieces(qi, j):
            qh = q_ref[0, qi * tq + c0:qi * tq + c0 + nc, lanes]
            first = lax.broadcasted_iota(jnp.int32, qh.shape, 1) < DIFF_HEAD_DIM
            zero = jnp.zeros_like(qh)
            kj = k_ref[0, j * tk:j * tk + nk, lanes]
            out.append((_dot_nt(kj, jnp.where(first, qh, zero)), _dot_nt(kj, jnp.where(first, zero, qh))))
        return out

    def finalize(hd, row0, state):
        (_, l0, a0), (_, l1, a1) = state
        ot = a0 / l0 - lam * (a1 / l1)
        ms = jnp.mean(ot * ot, axis=0, keepdims=True)
        ot = ot * lax.rsqrt(ms + SUBLN_EPS) * g_col * out_scale
        o_ref[0, row0:row0 + ot.shape[1], hd * HEAD_W:(hd + 1) * HEAD_W] = ot.T.astype(o_ref.dtype)

    def softmax_pv(hd, qi, j, s_cur, state):
        vt_aug = jnp.concatenate([vt_ref[0, j, hd * HEAD_W:(hd + 1) * HEAD_W, :], ones_rows], axis=0)
        new_state = []
        for (c0, nc, nk, masked), s_piece in zip(pieces(qi, j), s_cur):
            piece_state = []
            for c in range(2):
                s = s_piece[c]
                if masked:
                    key = lax.broadcasted_iota(jnp.int32, (nk, nc), 0)
                    qry = lax.broadcasted_iota(jnp.int32, (nk, nc), 1) + c0
                    s = jnp.where(key <= qry, s, NEG_INF)
                m_new = jnp.max(s, axis=0, keepdims=True)
                if j > 0:
                    m_old, l_old, acc_old = (a[:, c0:c0 + nc] for a in state[c])
                    m_new = jnp.maximum(m_old, m_new)
                    alpha = jnp.exp2(m_old - m_new)
                pv = _dot(vt_aug[:, 0:nk], jnp.exp2(s - m_new).astype(BF16))
                acc, l = pv[:HEAD_W], pv[HEAD_W:HEAD_W + 1]
                if j > 0:
                    acc, l = alpha * acc_old + acc, alpha * l_old + l
                piece_state.append((m_new, l, acc))
            if masked:
                finalize(hd, qi * tq + c0, piece_state)
            else:
                new_state = piece_state
        return new_state

    pairs = [(qi, j) for qi in range(n_q) for j in range((qi * tq) // tk + 1)]
    s_next = [scores(hd, *pairs[0]) for hd in heads]
    state = [None for _ in heads]
    for t, (qi, j) in enumerate(pairs):
        s_cur = s_next
        if t + 1 < len(pairs):
            s_next = [scores(hd, *pairs[t + 1]) for hd in heads]
        state = [softmax_pv(hd, qi, j, s_cur[hd], state[hd]) for hd in heads]


def _diff_attn(lam_rows, q, k, vt, subln_g_col, lambda_init, layer):
    b, s, w = q.shape
    nk, tk = vt.shape[1], vt.shape[3]
    assert Q_TILE == tk, "diagonal tiles are assumed square"
    gw = ATTN_HEADS_PER_STEP * HEAD_W
    kern = functools.partial(_diff_attn_kernel, out_scale=1.0 - lambda_init, lambda_init=lambda_init)
    return pl.pallas_call(
        kern,
        grid=(b, DIFF_HEADS // ATTN_HEADS_PER_STEP),
        in_specs=[
            _layer_resident((4, DIFF_HEAD_DIM), layer),
            pl.BlockSpec((1, s, gw), lambda i, h: (i, 0, h)),
            pl.BlockSpec((1, s, gw), lambda i, h: (i, 0, h)),
            pl.BlockSpec((1, nk, gw, tk), lambda i, h: (i, 0, h, 0)),
            _layer_resident((HEAD_W, 1), layer),
        ],
        out_specs=pl.BlockSpec((1, s, gw), lambda i, h: (i, 0, h)),
        out_shape=jax.ShapeDtypeStruct((b, s, w), BF16),
        compiler_params=pltpu.CompilerParams(
            dimension_semantics=("arbitrary", "arbitrary"), vmem_limit_bytes=VMEM_LIMIT),
        name="diff_attn",
    )(lam_rows, q, k, vt, subln_g_col)


def _mix_xattn_kernel(h_ref, a_ref, c_ref, wmo_ref, g_ref, wq_ref, kv_ref, wo_ref, o_ref):
    aw = ATTN_WIDTH
    scale = X_HEAD_DIM ** -0.5
    h1 = h_ref[0] + _dot(a_ref[0], wmo_ref[0, 0:aw, :]) + _dot(c_ref[0], wmo_ref[0, aw:, :])
    xn = _rms(h1, g_ref[0], NORM_EPS).astype(BF16)
    q = (_dot(xn, wq_ref[0]) * scale).astype(BF16)
    heads = []
    for hh in range(X_HEADS):
        lo, hi = hh * X_HEAD_DIM, (hh + 1) * X_HEAD_DIM
        s = _dot_nt(q[:, lo:hi], kv_ref[0, 0, :, lo:hi])
        p = jnp.exp(s - jnp.max(s, axis=-1, keepdims=True))
        l = jnp.sum(p, axis=-1, keepdims=True)
        o = _dot(p.astype(BF16), kv_ref[0, 0, :, D_MODEL + lo:D_MODEL + hi]) / l
        heads.append(o.astype(BF16))
    o_all = jnp.concatenate(heads, axis=-1)
    o_ref[0] = h1 + _dot(o_all, wo_ref[0])


def _mix_xattn(h, attn, conv, w_mo, g, w_xq, kv, w_xo, layer, w_layer, casts):
    b, s, d = h.shape
    tm = ROW_TILE
    n_j = s // tm
    cast_in, cast_out, cast_shapes = _cast_plan(casts, b * n_j, lambda i, j: i * n_j + j)
    in_specs = [
        pl.BlockSpec((1, tm, d), lambda i, j: (i, j, 0)),
        pl.BlockSpec((1, tm, ATTN_WIDTH), lambda i, j: (i, j, 0)),
        pl.BlockSpec((1, tm, CONV_WIDTH_CH), lambda i, j: (i, j, 0)),
        _layer_resident((d, d), w_layer),
        _layer_resident((1, d), layer),
        _layer_resident((d, d), w_layer),
        pl.BlockSpec((1, 1, MEM_LEN, 2 * d), lambda i, j: (layer, i, 0, 0)),
        _layer_resident((d, d), w_layer),
    ]
    return pl.pallas_call(
        _with_casts(_mix_xattn_kernel, len(in_specs), 1, len(casts)),
        grid=(b, n_j),
        in_specs=in_specs + cast_in,
        out_specs=[pl.BlockSpec((1, tm, d), lambda i, j: (i, j, 0))] + cast_out,
        out_shape=[jax.ShapeDtypeStruct((b, s, d), F32)] + cast_shapes,
        compiler_params=pltpu.CompilerParams(
            dimension_semantics=("arbitrary", "arbitrary"), vmem_limit_bytes=VMEM_LIMIT),
        name="mix_xattn",
    )(h, attn, conv, w_mo, g, w_xq, kv, w_xo, *(w for w, _ in casts))


def _ffn_kernel(h_ref, g_ref, w1_ref, w2_ref, fg_ref, o_ref, acc_ref, *, final_norm):
    h = h_ref[...]
    xn = _rms(h, g_ref[0], NORM_EPS).astype(BF16)
    for c in range(D_FF // FF_CHUNK):
        lo, hi = c * FF_CHUNK, (c + 1) * FF_CHUNK
        f = jnp.square(jnp.maximum(_dot(xn, w1_ref[0, :, lo:hi]), 0.0)).astype(BF16)
        part = _dot(f, w2_ref[0, lo:hi, :])
        if c == 0:
            acc_ref[...] = part
        else:
            acc_ref[...] += part
    out = h + acc_ref[...]
    if final_norm:
        out = _rms(out, fg_ref[...], NORM_EPS)
    o_ref[...] = out


def _ffn(h2d, g, w1, w2, final_g, final_norm, layer, w_layer):
    n, d = h2d.shape
    tm = ROW_TILE
    return pl.pallas_call(
        functools.partial(_ffn_kernel, final_norm=final_norm),
        grid=(n // tm,),
        in_specs=[
            pl.BlockSpec((tm, d), lambda i: (i, 0)),
            _layer_resident((1, d), layer),
            _layer_resident((d, D_FF), w_layer),
            _layer_resident((D_FF, d), w_layer),
            _resident((1, d)),
        ],
        out_specs=pl.BlockSpec((tm, d), lambda i: (i, 0)),
        out_shape=jax.ShapeDtypeStruct((n, d), F32),
        scratch_shapes=[pltpu.VMEM((tm, d), F32)],
        compiler_params=pltpu.CompilerParams(
            dimension_semantics=("arbitrary",), vmem_limit_bytes=VMEM_LIMIT),
        name="ffn",
    )(h2d, g, w1, w2, final_g)


def _mem_kv_kernel(m_ref, g_ref, w_ref, o_ref):
    mn = _rms(m_ref[...], g_ref[...], NORM_EPS).astype(BF16)
    o_ref[0] = _dot(mn, w_ref[0]).astype(BF16)


def _mem_kv(mem2d, g, w_xkv, casts):
    n, d = mem2d.shape
    depth, _, n_cols = w_xkv.shape
    tm = ROW_TILE
    n_i = n // tm
    cast_in, cast_out, cast_shapes = _cast_plan(casts, depth * n_i, lambda l, i: l * n_i + i)
    in_specs = [
        pl.BlockSpec((tm, d), lambda l, i: (i, 0)),
        pl.BlockSpec((1, d), lambda l, i: (0, 0)),
        pl.BlockSpec((1, d, n_cols), lambda l, i: (l, 0, 0)),
    ]
    return pl.pallas_call(
        _with_casts(_mem_kv_kernel, len(in_specs), 1, len(casts)),
        grid=(depth, n_i),
        in_specs=in_specs + cast_in,
        out_specs=[pl.BlockSpec((1, tm, n_cols), lambda l, i: (l, i, 0))] + cast_out,
        out_shape=[jax.ShapeDtypeStruct((depth, n, n_cols), BF16)] + cast_shapes,
        compiler_params=pltpu.CompilerParams(
            dimension_semantics=("arbitrary", "arbitrary"), vmem_limit_bytes=VMEM_LIMIT),
        name="mem_kv",
    )(mem2d, g, w_xkv, *(w for w, _ in casts))


def kernel(x, mem, positions, norm_mix_g, w_in, lam_q1, lam_k1, lam_q2, lam_k2, subln_g, conv_w,
           w_mix_out, norm_x_g, mem_norm_g, w_xq, w_xkv, w_xo, norm_ffn_g, w_ff1, w_ff2, final_g):
    b, s, d = x.shape
    aw = ATTN_WIDTH
    pos3 = positions.reshape(b, 1, s)
    inv_freq = (ROPE_THETA ** (-jnp.arange(0, ROT_DIM, 2, dtype=F32) / ROT_DIM)).reshape(ROT_DIM // 2, 1)
    w_vt_b = jnp.swapaxes(lax.optimization_barrier(w_in[:, :, 2 * aw:3 * aw]), 1, 2).astype(BF16)
    g_mix, g_x, g_ffn = (g.reshape(DEPTH, 1, d) for g in (norm_mix_g, norm_x_g, norm_ffn_g))
    lam_rows = jnp.stack([lam_q1, lam_k1, lam_q2, lam_k2], axis=1).astype(F32)
    subln_col = subln_g.reshape(DEPTH, HEAD_W, 1)

    kv, w_in_b = _mem_kv(mem.reshape(b * MEM_LEN, d), mem_norm_g.reshape(1, d), w_xkv.astype(BF16),
                         casts=[(w_in, 0)])
    kv = kv.reshape(DEPTH, b, MEM_LEN, 2 * d)
    h = x
    for l in range(DEPTH):
        lambda_init = 0.8 - 0.6 * math.exp(-0.3 * l)
        q, k, vt, conv, w_mo_b, w_xq_b, w_xo_b = _in_proj(
            h, pos3, inv_freq, g_mix, w_in_b, 0, w_vt_b, conv_w, l,
            casts=[(w_mix_out, l), (w_xq, l), (w_xo, l)])
        attn = _diff_attn(lam_rows, q, k, vt, subln_col, lambda_init, l)
        next_w_in = [(w_in, l + 1)] if l + 1 < DEPTH else []
        h, w_ff1_b, w_ff2_b, *w_in_next = _mix_xattn(
            h, attn, conv, w_mo_b, g_x, w_xq_b, kv, w_xo_b, l, 0,
            casts=[(w_ff1, l), (w_ff2, l)] + next_w_in)
        if w_in_next:
            w_in_b = w_in_next[0]
        h = _ffn(h.reshape(b * s, d), g_ffn, w_ff1_b, w_ff2_b, final_g.reshape(1, d),
                 l == DEPTH - 1, l, 0).reshape(b, s, d)
    return h
```

```python
import functools
import math

import jax
import jax.numpy as jnp
from jax import lax
from jax.experimental import pallas as pl
from jax.experimental.pallas import tpu as pltpu

D_MODEL = 1024
DEPTH = 2
MEM_LEN = 256
ATTN_WIDTH = D_MODEL // 2
CONV_WIDTH_CH = D_MODEL - ATTN_WIDTH
DIFF_HEAD_DIM = 64
DIFF_HEADS = ATTN_WIDTH // (2 * DIFF_HEAD_DIM)
HEAD_W = 2 * DIFF_HEAD_DIM
CONV_K = 3
ROT_DIM = DIFF_HEAD_DIM // 4
ROPE_THETA = 500000.0
X_HEADS = 4
X_HEAD_DIM = D_MODEL // X_HEADS
D_FF = 4 * D_MODEL
NEG_INF = -1e30
NORM_EPS = 1e-6
SUBLN_EPS = 1e-5

LANES = 128
SUBLANES = 8
VMEM_LIMIT = 56 * 1024 * 1024

ROW_TILE = 1024
Q_TILE = 512
ATTN_HEADS_PER_STEP = 2
SUM_ROWS = 16
FF_CHUNK = 1024

BF16 = jnp.bfloat16
F32 = jnp.float32


def _rms(x, g, eps):
    return x * lax.rsqrt(jnp.mean(x * x, axis=-1, keepdims=True) + eps) * g


def _dot(a, b):
    return jnp.dot(a, b, preferred_element_type=F32)


def _dot_nt(a, b):
    return lax.dot_general(a, b, (((1,), (1,)), ((), ())), preferred_element_type=F32)


def _resident(shape):
    zeros = (0,) * len(shape)
    return pl.BlockSpec(shape, lambda *_: zeros, pipeline_mode=pl.Buffered(1))


def _layer_resident(shape, layer):
    index = (layer,) + (0,) * len(shape)
    return pl.BlockSpec((1,) + tuple(shape), lambda *_: index, pipeline_mode=pl.Buffered(1))


def _cast_plan(casts, n_steps, step_of):
    in_specs, out_specs, out_shapes = [], [], []
    for w, layer in casts:
        r, c = w.shape[1:]
        assert r % n_steps == 0, (r, n_steps)
        rows = r // n_steps
        in_specs.append(pl.BlockSpec((1, rows, c), lambda *g, layer=layer: (layer, step_of(*g), 0)))
        out_specs.append(pl.BlockSpec((1, rows, c), lambda *g: (0, step_of(*g), 0)))
        out_shapes.append(jax.ShapeDtypeStruct((1, r, c), BF16))
    return in_specs, out_specs, out_shapes


def _with_casts(body, n_in, n_out, n_cast):
    def kernel(*refs):
        ins, cast_in = refs[:n_in], refs[n_in:n_in + n_cast]
        outs = refs[n_in + n_cast:n_in + n_cast + n_out]
        cast_out = refs[n_in + n_cast + n_out:n_in + 2 * n_cast + n_out]
        scratch = refs[n_in + 2 * n_cast + n_out:]
        for src, dst in zip(cast_in, cast_out):
            dst[...] = src[...].astype(BF16)
        body(*ins, *outs, *scratch)
    return kernel


def _in_proj_kernel(pos_ref, inv_freq_ref, x_ref, g_ref, w_ref, wvt_ref, cw_ref,
                    q_ref, k_ref, vt_ref, c_ref, carry_ref):
    tm = x_ref.shape[1]
    aw = ATTN_WIDTH

    @pl.when(pl.program_id(1) == 0)
    def _():
        carry_ref[...] = jnp.zeros_like(carry_ref)

    xn = _rms(x_ref[0], g_ref[0], NORM_EPS).astype(BF16)

    half = ROT_DIM // 2
    ang_t = inv_freq_ref[...] * pos_ref[0].astype(F32)
    cos_t, sin_t = jnp.cos(ang_t), jnp.sin(ang_t)
    rest = DIFF_HEAD_DIM - ROT_DIM
    ones, zeros = jnp.ones((rest, tm), F32), jnp.zeros((rest, tm), F32)
    cos = jnp.concatenate([cos_t, cos_t, ones] * 2, axis=0).T
    sin = jnp.concatenate([-sin_t, sin_t, zeros] * 2, axis=0).T
    lane = lax.broadcasted_iota(jnp.int32, (tm, HEAD_W), 1)
    pairs_up = (lane & (DIFF_HEAD_DIM - 1)) < half

    def rotate(t):
        up = pltpu.roll(t, LANES - half, axis=1)
        dn = pltpu.roll(t, half, axis=1)
        return t * cos + jnp.where(pairs_up, up, dn) * sin

    scale = math.log2(math.e) * DIFF_HEAD_DIM ** -0.5
    q = _dot(xn, w_ref[0, :, 0:aw])
    k = _dot(xn, w_ref[0, :, aw:2 * aw])
    for h in range(DIFF_HEADS):
        lo, hi = h * HEAD_W, (h + 1) * HEAD_W
        q_ref[0, :, lo:hi] = (rotate(q[:, lo:hi]) * scale).astype(BF16)
        k_ref[0, :, lo:hi] = rotate(k[:, lo:hi]).astype(BF16)

    cwid = CONV_WIDTH_CH
    c_gate = _dot(xn, w_ref[0, :, 3 * aw + cwid:3 * aw + 2 * cwid])
    hc = _dot(xn, w_ref[0, :, 3 * aw + 2 * cwid:3 * aw + 3 * cwid])
    b_gate = _dot(xn, w_ref[0, :, 3 * aw:3 * aw + cwid])
    u = c_gate * hc
    prev = carry_ref[...]
    row = lax.broadcasted_iota(jnp.int32, u.shape, 0)
    last = prev[SUBLANES - 1:SUBLANES, :]
    last2 = prev[SUBLANES - 2:SUBLANES - 1, :]
    u1 = jnp.where(row == 0, last, pltpu.roll(u, 1, axis=0))
    u2 = jnp.where(row == 0, last2, jnp.where(row == 1, last, pltpu.roll(u, 2, axis=0)))
    cw = cw_ref[0]
    y = cw[0:1, :] * u2 + cw[1:2, :] * u1 + cw[2:3, :] * u
    c_ref[0] = (b_gate * y).astype(BF16)
    carry_ref[...] = u[tm - SUBLANES:tm, :]

    vt = _dot_nt(wvt_ref[0], xn).astype(BF16)
    tk = vt_ref.shape[3]
    for i in range(tm // tk):
        vt_ref[0, i] = vt[:, i * tk:(i + 1) * tk]


def _in_proj(h, pos3, inv_freq, g, w_in, w_in_layer, w_vt, conv_w, layer, casts):
    b, s, d = h.shape
    tm = ROW_TILE
    n_cols = w_in.shape[2]
    act = jax.ShapeDtypeStruct((b, s, ATTN_WIDTH), BF16)
    act_spec = pl.BlockSpec((1, tm, ATTN_WIDTH), lambda i, j: (i, j, 0))
    tk = Q_TILE
    vt = jax.ShapeDtypeStruct((b, s // tk, ATTN_WIDTH, tk), BF16)
    vt_spec = pl.BlockSpec((1, tm // tk, ATTN_WIDTH, tk), lambda i, j: (i, j, 0, 0))
    n_j = s // tm
    cast_in, cast_out, cast_shapes = _cast_plan(casts, b * n_j, lambda i, j: i * n_j + j)
    in_specs = [
        pl.BlockSpec((1, 1, tm), lambda i, j: (i, 0, j)),
        _resident((ROT_DIM // 2, 1)),
        pl.BlockSpec((1, tm, d), lambda i, j: (i, j, 0)),
        _layer_resident((1, d), layer),
        _layer_resident((d, n_cols), w_in_layer),
        _layer_resident((ATTN_WIDTH, d), layer),
        _layer_resident((CONV_K, CONV_WIDTH_CH), layer),
    ]
    return pl.pallas_call(
        _with_casts(_in_proj_kernel, len(in_specs), 4, len(casts)),
        grid=(b, n_j),
        in_specs=in_specs + cast_in,
        out_specs=[act_spec, act_spec, vt_spec, act_spec] + cast_out,
        out_shape=[act, act, vt, act] + cast_shapes,
        scratch_shapes=[pltpu.VMEM((SUBLANES, CONV_WIDTH_CH), F32)],
        compiler_params=pltpu.CompilerParams(
            dimension_semantics=("arbitrary", "arbitrary"), vmem_limit_bytes=VMEM_LIMIT),
        name="in_proj",
    )(pos3, inv_freq, h, g, w_in, w_vt, conv_w, *(w for w, _ in casts))


def _diff_attn_kernel(lam_ref, q_ref, k_ref, vt_ref, g_ref, o_ref, *, out_scale, lambda_init):
    tq = Q_TILE
    tk = vt_ref.shape[3]
    n_q = q_ref.shape[1] // tq
    heads = range(q_ref.shape[2] // HEAD_W)
    lam_rows = lam_ref[0]
    lam = (jnp.exp(jnp.sum(lam_rows[0:1, :] * lam_rows[1:2, :], axis=-1, keepdims=True))
           - jnp.exp(jnp.sum(lam_rows[2:3, :] * lam_rows[3:4, :], axis=-1, keepdims=True))
           + lambda_init)
    g_col = g_ref[0]
    ones_rows = jnp.ones((SUM_ROWS, tk), BF16)
    hq = tq // 2

    def pieces(qi, j):
        if (j + 1) * tk >= (qi + 1) * tq:
            return [(0, hq, hq, True), (hq, tq - hq, tk, True)]
        return [(0, tq, tk, False)]

    def scores(hd, qi, j):
        lanes = slice(hd * HEAD_W, (hd + 1) * HEAD_W)
        out = []
        for c0, nc, nk, _ in pieces(qi, j):
            qh = q_ref[0, qi * tq + c0:qi * tq + c0 + nc, lanes]
            first = lax.broadcasted_iota(jnp.int32, qh.shape, 1) < DIFF_HEAD_DIM
            zero = jnp.zeros_like(qh)
            kj = k_ref[0, j * tk:j * tk + nk, lanes]
            out.append((_dot_nt(kj, jnp.where(first, qh, zero)), _dot_nt(kj, jnp.where(first, zero, qh))))
        return out

    def finalize(hd, row0, state):
        (_, l0, a0), (_, l1, a1) = state
        ot = a0 / l0 - lam * (a1 / l1)
        ms = jnp.mean(ot * ot, axis=0, keepdims=True)
        ot = ot * lax.rsqrt(ms + SUBLN_EPS) * g_col * out_scale
        o_ref[0, row0:row0 + ot.shape[1], hd * HEAD_W:(hd + 1) * HEAD_W] = ot.T.astype(o_ref.dtype)

    def softmax_pv(hd, qi, j, s_cur, state):
        vt_aug = jnp.concatenate([vt_ref[0, j, hd * HEAD_W:(hd + 1) * HEAD_W, :], ones_rows], axis=0)
        new_state = []
        for (c0, nc, nk, masked), s_piece in zip(pieces(qi, j), s_cur):
            piece_state = []
            for c in range(2):
                s = s_piece[c].astype(BF16)
                if masked:
                    key = lax.broadcasted_iota(jnp.int32, (nk, nc), 0)
                    qry = lax.broadcasted_iota(jnp.int32, (nk, nc), 1) + c0
                    s = jnp.where(key <= qry, s, jnp.asarray(NEG_INF, BF16))
                m_new = jnp.max(s, axis=0, keepdims=True).astype(F32)
                if j > 0:
                    m_old, l_old, acc_old = (a[:, c0:c0 + nc] for a in state[c])
                    m_new = jnp.maximum(m_old, m_new)
                    alpha = jnp.exp2(m_old - m_new)
                pv = _dot(vt_aug[:, 0:nk], jnp.exp2(s - m_new.astype(BF16)))
                acc, l = pv[:HEAD_W], pv[HEAD_W:HEAD_W + 1]
                if j > 0:
                    acc, l = alpha * acc_old + acc, alpha * l_old + l
                piece_state.append((m_new, l, acc))
            if masked:
                finalize(hd, qi * tq + c0, piece_state)
            else:
                new_state = piece_state
        return new_state

    pairs = [(qi, j) for qi in range(n_q) for j in range((qi * tq) // tk + 1)]
    s_next = [scores(hd, *pairs[0]) for hd in heads]
    state = [None for _ in heads]
    for t, (qi, j) in enumerate(pairs):
        for hd in heads:
            s_cur = s_next[hd]
            if t + 1 < len(pairs):
                s_next[hd] = scores(hd, *pairs[t + 1])
            state[hd] = softmax_pv(hd, qi, j, s_cur, state[hd])


def _diff_attn(lam_rows, q, k, vt, subln_g_col, lambda_init, layer):
    b, s, w = q.shape
    nk, tk = vt.shape[1], vt.shape[3]
    assert Q_TILE == tk, "diagonal tiles are assumed square"
    gw = ATTN_HEADS_PER_STEP * HEAD_W
    kern = functools.partial(_diff_attn_kernel, out_scale=1.0 - lambda_init, lambda_init=lambda_init)
    return pl.pallas_call(
        kern,
        grid=(b, DIFF_HEADS // ATTN_HEADS_PER_STEP),
        in_specs=[
            _layer_resident((4, DIFF_HEAD_DIM), layer),
            pl.BlockSpec((1, s, gw), lambda i, h: (i, 0, h)),
            pl.BlockSpec((1, s, gw), lambda i, h: (i, 0, h)),
            pl.BlockSpec((1, nk, gw, tk), lambda i, h: (i, 0, h, 0)),
            _layer_resident((HEAD_W, 1), layer),
        ],
        out_specs=pl.BlockSpec((1, s, gw), lambda i, h: (i, 0, h)),
        out_shape=jax.ShapeDtypeStruct((b, s, w), BF16),
        compiler_params=pltpu.CompilerParams(
            dimension_semantics=("arbitrary", "arbitrary"), vmem_limit_bytes=VMEM_LIMIT),
        name="diff_attn",
    )(lam_rows, q, k, vt, subln_g_col)


def _mix_xattn_kernel(h_ref, a_ref, c_ref, wmo_ref, g_ref, wq_ref, kv_ref, wo_ref, o_ref):
    aw = ATTN_WIDTH
    scale = X_HEAD_DIM ** -0.5
    h1 = h_ref[0] + _dot(a_ref[0], wmo_ref[0, 0:aw, :]) + _dot(c_ref[0], wmo_ref[0, aw:, :])
    xn = _rms(h1, g_ref[0], NORM_EPS).astype(BF16)
    q = (_dot(xn, wq_ref[0]) * scale).astype(BF16)
    heads = []
    for hh in range(X_HEADS):
        lo, hi = hh * X_HEAD_DIM, (hh + 1) * X_HEAD_DIM
        s = _dot_nt(q[:, lo:hi], kv_ref[0, 0, :, lo:hi])
        p = jnp.exp(s - jnp.max(s, axis=-1, keepdims=True))
        l = jnp.sum(p, axis=-1, keepdims=True)
        o = _dot(p.astype(BF16), kv_ref[0, 0, :, D_MODEL + lo:D_MODEL + hi]) / l
        heads.append(o.astype(BF16))
    o_all = jnp.concatenate(heads, axis=-1)
    o_ref[0] = h1 + _dot(o_all, wo_ref[0])


def _mix_xattn(h, attn, conv, w_mo, g, w_xq, kv, w_xo, layer, w_layer, casts):
    b, s, d = h.shape
    tm = ROW_TILE
    n_j = s // tm
    cast_in, cast_out, cast_shapes = _cast_plan(casts, b * n_j, lambda i, j: i * n_j + j)
    in_specs = [
        pl.BlockSpec((1, tm, d), lambda i, j: (i, j, 0)),
        pl.BlockSpec((1, tm, ATTN_WIDTH), lambda i, j: (i, j, 0)),
        pl.BlockSpec((1, tm, CONV_WIDTH_CH), lambda i, j: (i, j, 0)),
        _layer_resident((d, d), w_layer),
        _layer_resident((1, d), layer),
        _layer_resident((d, d), w_layer),
        pl.BlockSpec((1, 1, MEM_LEN, 2 * d), lambda i, j: (layer, i, 0, 0)),
        _layer_resident((d, d), w_layer),
    ]
    return pl.pallas_call(
        _with_casts(_mix_xattn_kernel, len(in_specs), 1, len(casts)),
        grid=(b, n_j),
        in_specs=in_specs + cast_in,
        out_specs=[pl.BlockSpec((1, tm, d), lambda i, j: (i, j, 0))] + cast_out,
        out_shape=[jax.ShapeDtypeStruct((b, s, d), F32)] + cast_shapes,
        compiler_params=pltpu.CompilerParams(
            dimension_semantics=("arbitrary", "arbitrary"), vmem_limit_bytes=VMEM_LIMIT),
        name="mix_xattn",
    )(h, attn, conv, w_mo, g, w_xq, kv, w_xo, *(w for w, _ in casts))


def _ffn_kernel(h_ref, g_ref, w1_ref, w2_ref, fg_ref, o_ref, acc_ref, *, final_norm):
    h = h_ref[...]
    xn = _rms(h, g_ref[0], NORM_EPS).astype(BF16)
    for c in range(D_FF // FF_CHUNK):
        lo, hi = c * FF_CHUNK, (c + 1) * FF_CHUNK
        f = jnp.square(jnp.maximum(_dot(xn, w1_ref[0, :, lo:hi]), 0.0)).astype(BF16)
        part = _dot(f, w2_ref[0, lo:hi, :])
        if c == 0:
            acc_ref[...] = part
        else:
            acc_ref[...] += part
    out = h + acc_ref[...]
    if final_norm:
        out = _rms(out, fg_ref[...], NORM_EPS)
    o_ref[...] = out


def _ffn(h2d, g, w1, w2, final_g, final_norm, layer, w_layer):
    n, d = h2d.shape
    tm = ROW_TILE
    return pl.pallas_call(
        functools.partial(_ffn_kernel, final_norm=final_norm),
        grid=(n // tm,),
        in_specs=[
            pl.BlockSpec((tm, d), lambda i: (i, 0)),
            _layer_resident((1, d), layer),
            _layer_resident((d, D_FF), w_layer),
            _layer_resident((D_FF, d), w_layer),
            _resident((1, d)),
        ],
        out_specs=pl.BlockSpec((tm, d), lambda i: (i, 0)),
        out_shape=jax.ShapeDtypeStruct((n, d), F32),
        scratch_shapes=[pltpu.VMEM((tm, d), F32)],
        compiler_params=pltpu.CompilerParams(
            dimension_semantics=("arbitrary",), vmem_limit_bytes=VMEM_LIMIT),
        name="ffn",
    )(h2d, g, w1, w2, final_g)


def _mem_kv_kernel(m_ref, g_ref, w_ref, o_ref):
    mn = _rms(m_ref[...], g_ref[...], NORM_EPS).astype(BF16)
    o_ref[0] = _dot(mn, w_ref[0]).astype(BF16)


def _mem_kv(mem2d, g, w_xkv, casts):
    n, d = mem2d.shape
    depth, _, n_cols = w_xkv.shape
    tm = ROW_TILE
    n_i = n // tm
    cast_in, cast_out, cast_shapes = _cast_plan(casts, depth * n_i, lambda l, i: l * n_i + i)
    in_specs = [
        pl.BlockSpec((tm, d), lambda l, i: (i, 0)),
        pl.BlockSpec((1, d), lambda l, i: (0, 0)),
        pl.BlockSpec((1, d, n_cols), lambda l, i: (l, 0, 0)),
    ]
    return pl.pallas_call(
        _with_casts(_mem_kv_kernel, len(in_specs), 1, len(casts)),
        grid=(depth, n_i),
        in_specs=in_specs + cast_in,
        out_specs=[pl.BlockSpec((1, tm, n_cols), lambda l, i: (l, i, 0))] + cast_out,
        out_shape=[jax.ShapeDtypeStruct((depth, n, n_cols), BF16)] + cast_shapes,
        compiler_params=pltpu.CompilerParams(
            dimension_semantics=("arbitrary", "arbitrary"), vmem_limit_bytes=VMEM_LIMIT),
        name="mem_kv",
    )(mem2d, g, w_xkv, *(w for w, _ in casts))


def kernel(x, mem, positions, norm_mix_g, w_in, lam_q1, lam_k1, lam_q2, lam_k2, subln_g, conv_w,
           w_mix_out, norm_x_g, mem_norm_g, w_xq, w_xkv, w_xo, norm_ffn_g, w_ff1, w_ff2, final_g):
    b, s, d = x.shape
    aw = ATTN_WIDTH
    pos3 = positions.reshape(b, 1, s)
    inv_freq = (ROPE_THETA ** (-jnp.arange(0, ROT_DIM, 2, dtype=F32) / ROT_DIM)).reshape(ROT_DIM // 2, 1)
    w_vt_b = jnp.swapaxes(lax.optimization_barrier(w_in[:, :, 2 * aw:3 * aw]), 1, 2).astype(BF16)
    g_mix, g_x, g_ffn = (g.reshape(DEPTH, 1, d) for g in (norm_mix_g, norm_x_g, norm_ffn_g))
    lam_rows = jnp.stack([lam_q1, lam_k1, lam_q2, lam_k2], axis=1).astype(F32)
    subln_col = subln_g.reshape(DEPTH, HEAD_W, 1)

    kv, w_in_b = _mem_kv(mem.reshape(b * MEM_LEN, d), mem_norm_g.reshape(1, d), w_xkv.astype(BF16),
                         casts=[(w_in, 0)])
    kv = kv.reshape(DEPTH, b, MEM_LEN, 2 * d)
    h = x
    for l in range(DEPTH):
        lambda_init = 0.8 - 0.6 * math.exp(-0.3 * l)
        q, k, vt, conv, w_mo_b, w_xq_b, w_xo_b = _in_proj(
            h, pos3, inv_freq, g_mix, w_in_b, 0, w_vt_b, conv_w, l,
            casts=[(w_mix_out, l), (w_xq, l), (w_xo, l)])
        attn = _diff_attn(lam_rows, q, k, vt, subln_col, lambda_init, l)
        next_w_in = [(w_in, l + 1)] if l + 1 < DEPTH else []
        h, w_ff1_b, w_ff2_b, *w_in_next = _mix_xattn(
            h, attn, conv, w_mo_b, g_x, w_xq_b, kv, w_xo_b, l, 0,
            casts=[(w_ff1, l), (w_ff2, l)] + next_w_in)
        if w_in_next:
            w_in_b = w_in_next[0]
        h = _ffn(h.reshape(b * s, d), g_ffn, w_ff1_b, w_ff2_b, final_g.reshape(1, d),
                 l == DEPTH - 1, l, 0).reshape(b, s, d)
    return h
```

```python
import functools
import math

import jax
import jax.numpy as jnp
from jax import lax
from jax.experimental import pallas as pl
from jax.experimental.pallas import tpu as pltpu

D_MODEL = 1024
DEPTH = 2
MEM_LEN = 256
ATTN_WIDTH = D_MODEL // 2
CONV_WIDTH_CH = D_MODEL - ATTN_WIDTH
DIFF_HEAD_DIM = 64
DIFF_HEADS = ATTN_WIDTH // (2 * DIFF_HEAD_DIM)
HEAD_W = 2 * DIFF_HEAD_DIM
CONV_K = 3
ROT_DIM = DIFF_HEAD_DIM // 4
ROPE_THETA = 500000.0
X_HEADS = 4
X_HEAD_DIM = D_MODEL // X_HEADS
D_FF = 4 * D_MODEL
NEG_INF = -1e30
NORM_EPS = 1e-6
SUBLN_EPS = 1e-5

LANES = 128
SUBLANES = 8
VMEM_LIMIT = 56 * 1024 * 1024

ROW_TILE = 1024
Q_TILE = 512
ATTN_HEADS_PER_STEP = 2
SUM_ROWS = 16
FF_CHUNK = 1024

BF16 = jnp.bfloat16
F32 = jnp.float32


def _rms(x, g, eps):
    return x * lax.rsqrt(jnp.mean(x * x, axis=-1, keepdims=True) + eps) * g


def _dot(a, b):
    return jnp.dot(a, b, preferred_element_type=F32)


def _dot_nt(a, b):
    return lax.dot_general(a, b, (((1,), (1,)), ((), ())), preferred_element_type=F32)


def _resident(shape):
    zeros = (0,) * len(shape)
    return pl.BlockSpec(shape, lambda *_: zeros, pipeline_mode=pl.Buffered(1))


def _layer_resident(shape, layer):
    index = (layer,) + (0,) * len(shape)
    return pl.BlockSpec((1,) + tuple(shape), lambda *_: index, pipeline_mode=pl.Buffered(1))


def _cast_plan(casts, n_steps, step_of):
    in_specs, out_specs, out_shapes = [], [], []
    for w, layer in casts:
        r, c = w.shape[1:]
        assert r % n_steps == 0, (r, n_steps)
        rows = r // n_steps
        in_specs.append(pl.BlockSpec((1, rows, c), lambda *g, layer=layer: (layer, step_of(*g), 0)))
        out_specs.append(pl.BlockSpec((1, rows, c), lambda *g: (0, step_of(*g), 0)))
        out_shapes.append(jax.ShapeDtypeStruct((1, r, c), BF16))
    return in_specs, out_specs, out_shapes


def _with_casts(body, n_in, n_out, n_cast):
    def kernel(*refs):
        ins, cast_in = refs[:n_in], refs[n_in:n_in + n_cast]
        outs = refs[n_in + n_cast:n_in + n_cast + n_out]
        cast_out = refs[n_in + n_cast + n_out:n_in + 2 * n_cast + n_out]
        scratch = refs[n_in + 2 * n_cast + n_out:]
        for src, dst in zip(cast_in, cast_out):
            dst[...] = src[...].astype(BF16)
        body(*ins, *outs, *scratch)
    return kernel


def _in_proj_kernel(pos_ref, inv_freq_ref, x_ref, g_ref, w_ref, wqvt_ref, cw_ref,
                    qt_ref, k_ref, vt_ref, c_ref, carry_ref):
    tm = x_ref.shape[1]
    aw = ATTN_WIDTH
    tk = vt_ref.shape[3]

    @pl.when(pl.program_id(1) == 0)
    def _():
        carry_ref[...] = jnp.zeros_like(carry_ref)

    xn = _rms(x_ref[0], g_ref[0], NORM_EPS).astype(BF16)

    half = ROT_DIM // 2
    ang_t = inv_freq_ref[...] * pos_ref[0].astype(F32)
    cos_t, sin_t = jnp.cos(ang_t), jnp.sin(ang_t)
    rest = DIFF_HEAD_DIM - ROT_DIM
    ones, zeros = jnp.ones((rest, tm), F32), jnp.zeros((rest, tm), F32)
    cos_tab_t = jnp.concatenate([cos_t, cos_t, ones] * 2, axis=0)
    sin_tab_t = jnp.concatenate([-sin_t, sin_t, zeros] * 2, axis=0)
    cos, sin = cos_tab_t.T, sin_tab_t.T
    lane = lax.broadcasted_iota(jnp.int32, (tm, HEAD_W), 1)
    pairs_up = (lane & (DIFF_HEAD_DIM - 1)) < half

    def rotate(t):
        up = pltpu.roll(t, LANES - half, axis=1)
        dn = pltpu.roll(t, half, axis=1)
        return t * cos + jnp.where(pairs_up, up, dn) * sin

    def rotate_t(t):
        blocks = [t[c * DIFF_HEAD_DIM:(c + 1) * DIFF_HEAD_DIM] for c in range(2)]
        partner = jnp.concatenate([r for blk in blocks for r in (blk[half:ROT_DIM], blk[0:half], blk[ROT_DIM:])],
                                  axis=0)
        return t * cos_tab_t + partner * sin_tab_t

    k = _dot(xn, w_ref[0, :, aw:2 * aw])
    for h in range(DIFF_HEADS):
        lo, hi = h * HEAD_W, (h + 1) * HEAD_W
        k_ref[0, :, lo:hi] = rotate(k[:, lo:hi]).astype(BF16)

    scale = math.log2(math.e) * DIFF_HEAD_DIM ** -0.5
    qt = _dot_nt(wqvt_ref[0, 0:aw, :], xn)
    for h in range(DIFF_HEADS):
        lo, hi = h * HEAD_W, (h + 1) * HEAD_W
        qt_h = (rotate_t(qt[lo:hi]) * scale).astype(BF16)
        for i in range(tm // tk):
            qt_ref[0, i, lo:hi, :] = qt_h[:, i * tk:(i + 1) * tk]

    cwid = CONV_WIDTH_CH
    c_gate = _dot(xn, w_ref[0, :, 3 * aw + cwid:3 * aw + 2 * cwid])
    hc = _dot(xn, w_ref[0, :, 3 * aw + 2 * cwid:3 * aw + 3 * cwid])
    b_gate = _dot(xn, w_ref[0, :, 3 * aw:3 * aw + cwid])
    u = c_gate * hc
    prev = carry_ref[...]
    row = lax.broadcasted_iota(jnp.int32, u.shape, 0)
    last = prev[SUBLANES - 1:SUBLANES, :]
    last2 = prev[SUBLANES - 2:SUBLANES - 1, :]
    u1 = jnp.where(row == 0, last, pltpu.roll(u, 1, axis=0))
    u2 = jnp.where(row == 0, last2, jnp.where(row == 1, last, pltpu.roll(u, 2, axis=0)))
    cw = cw_ref[0]
    y = cw[0:1, :] * u2 + cw[1:2, :] * u1 + cw[2:3, :] * u
    c_ref[0] = (b_gate * y).astype(BF16)
    carry_ref[...] = u[tm - SUBLANES:tm, :]

    vt = _dot_nt(wqvt_ref[0, aw:2 * aw, :], xn).astype(BF16)
    for i in range(tm // tk):
        vt_ref[0, i] = vt[:, i * tk:(i + 1) * tk]


def _in_proj(h, pos3, inv_freq, g, w_in, w_in_layer, w_qvt, conv_w, layer, casts):
    b, s, d = h.shape
    tm = ROW_TILE
    n_cols = w_in.shape[2]
    act = jax.ShapeDtypeStruct((b, s, ATTN_WIDTH), BF16)
    act_spec = pl.BlockSpec((1, tm, ATTN_WIDTH), lambda i, j: (i, j, 0))
    tk = Q_TILE
    vt = jax.ShapeDtypeStruct((b, s // tk, ATTN_WIDTH, tk), BF16)
    vt_spec = pl.BlockSpec((1, tm // tk, ATTN_WIDTH, tk), lambda i, j: (i, j, 0, 0))
    n_j = s // tm
    cast_in, cast_out, cast_shapes = _cast_plan(casts, b * n_j, lambda i, j: i * n_j + j)
    in_specs = [
        pl.BlockSpec((1, 1, tm), lambda i, j: (i, 0, j)),
        _resident((ROT_DIM // 2, 1)),
        pl.BlockSpec((1, tm, d), lambda i, j: (i, j, 0)),
        _layer_resident((1, d), layer),
        _layer_resident((d, n_cols), w_in_layer),
        _layer_resident((2 * ATTN_WIDTH, d), layer),
        _layer_resident((CONV_K, CONV_WIDTH_CH), layer),
    ]
    return pl.pallas_call(
        _with_casts(_in_proj_kernel, len(in_specs), 4, len(casts)),
        grid=(b, n_j),
        in_specs=in_specs + cast_in,
        out_specs=[vt_spec, act_spec, vt_spec, act_spec] + cast_out,
        out_shape=[vt, act, vt, act] + cast_shapes,
        scratch_shapes=[pltpu.VMEM((SUBLANES, CONV_WIDTH_CH), F32)],
        compiler_params=pltpu.CompilerParams(
            dimension_semantics=("arbitrary", "arbitrary"), vmem_limit_bytes=VMEM_LIMIT),
        name="in_proj",
    )(pos3, inv_freq, h, g, w_in, w_qvt, conv_w, *(w for w, _ in casts))


def _diff_attn_kernel(lam_ref, qt_ref, k_ref, vt_ref, g_ref, o_ref, *, out_scale, lambda_init):
    tq = qt_ref.shape[3]
    tk = vt_ref.shape[3]
    n_q = qt_ref.shape[1]
    heads = range(qt_ref.shape[2] // HEAD_W)
    lam_rows = lam_ref[0]
    lam = (jnp.exp(jnp.sum(lam_rows[0:1, :] * lam_rows[1:2, :], axis=-1, keepdims=True))
           - jnp.exp(jnp.sum(lam_rows[2:3, :] * lam_rows[3:4, :], axis=-1, keepdims=True))
           + lambda_init)
    g_col = g_ref[0]
    ones_rows = jnp.ones((SUM_ROWS, tk), BF16)
    hq = tq // 2

    def pieces(qi, j):
        if (j + 1) * tk >= (qi + 1) * tq:
            return [(0, hq, hq, True), (hq, tq - hq, tk, True)]
        return [(0, tq, tk, False)]

    def scores(hd, qi, j):
        dims = slice(hd * HEAD_W, (hd + 1) * HEAD_W)
        out = []
        for c0, nc, nk, _ in pieces(qi, j):
            qt = qt_ref[0, qi, dims, c0:c0 + nc]
            first = lax.broadcasted_iota(jnp.int32, qt.shape, 0) < DIFF_HEAD_DIM
            zero = jnp.zeros_like(qt)
            kj = k_ref[0, j * tk:j * tk + nk, dims]
            out.append((_dot(kj, jnp.where(first, qt, zero)), _dot(kj, jnp.where(first, zero, qt))))
        return out

    def finalize(hd, row0, state):
        (_, l0, a0), (_, l1, a1) = state
        ot = a0 / l0 - lam * (a1 / l1)
        ms = jnp.mean(ot * ot, axis=0, keepdims=True)
        ot = ot * lax.rsqrt(ms + SUBLN_EPS) * g_col * out_scale
        o_ref[0, row0:row0 + ot.shape[1], hd * HEAD_W:(hd + 1) * HEAD_W] = ot.T.astype(o_ref.dtype)

    def softmax_pv(hd, qi, j, s_cur, state):
        vt_aug = jnp.concatenate([vt_ref[0, j, hd * HEAD_W:(hd + 1) * HEAD_W, :], ones_rows], axis=0)
        new_state = []
        for (c0, nc, nk, masked), s_piece in zip(pieces(qi, j), s_cur):
            piece_state = []
            for c in range(2):
                s = s_piece[c]
                if masked:
                    key = lax.broadcasted_iota(jnp.int32, (nk, nc), 0)
                    qry = lax.broadcasted_iota(jnp.int32, (nk, nc), 1) + c0
                    s = jnp.where(key <= qry, s, NEG_INF)
                m_new = jnp.max(s, axis=0, keepdims=True)
                if j > 0:
                    m_old, l_old, acc_old = (a[:, c0:c0 + nc] for a in state[c])
                    m_new = jnp.maximum(m_old, m_new)
                    alpha = jnp.exp2(m_old - m_new)
                pv = _dot(vt_aug[:, 0:nk], jnp.exp2(s - m_new).astype(BF16))
                acc, l = pv[:HEAD_W], pv[HEAD_W:HEAD_W + 1]
                if j > 0:
                    acc, l = alpha * acc_old + acc, alpha * l_old + l
                piece_state.append((m_new, l, acc))
            if masked:
                finalize(hd, qi * tq + c0, piece_state)
            else:
                new_state = piece_state
        return new_state

    pairs = [(qi, j) for qi in range(n_q) for j in range((qi * tq) // tk + 1)]
    s_next = [scores(hd, *pairs[0]) for hd in heads]
    state = [None for _ in heads]
    for t, (qi, j) in enumerate(pairs):
        s_cur = s_next
        if t + 1 < len(pairs):
            s_next = [scores(hd, *pairs[t + 1]) for hd in heads]
        state = [softmax_pv(hd, qi, j, s_cur[hd], state[hd]) for hd in heads]


def _diff_attn(lam_rows, qt, k, vt, subln_g_col, lambda_init, layer):
    b, s, w = k.shape
    nk, tk = vt.shape[1], vt.shape[3]
    assert qt.shape == vt.shape, "q^T and V^T tiles share one layout; diagonal tiles are square"
    gw = ATTN_HEADS_PER_STEP * HEAD_W
    kern = functools.partial(_diff_attn_kernel, out_scale=1.0 - lambda_init, lambda_init=lambda_init)
    return pl.pallas_call(
        kern,
        grid=(b, DIFF_HEADS // ATTN_HEADS_PER_STEP),
        in_specs=[
            _layer_resident((4, DIFF_HEAD_DIM), layer),
            pl.BlockSpec((1, nk, gw, tk), lambda i, h: (i, 0, h, 0)),
            pl.BlockSpec((1, s, gw), lambda i, h: (i, 0, h)),
            pl.BlockSpec((1, nk, gw, tk), lambda i, h: (i, 0, h, 0)),
            _layer_resident((HEAD_W, 1), layer),
        ],
        out_specs=pl.BlockSpec((1, s, gw), lambda i, h: (i, 0, h)),
        out_shape=jax.ShapeDtypeStruct((b, s, w), BF16),
        compiler_params=pltpu.CompilerParams(
            dimension_semantics=("arbitrary", "arbitrary"), vmem_limit_bytes=VMEM_LIMIT),
        name="diff_attn",
    )(lam_rows, qt, k, vt, subln_g_col)


def _mix_xattn_kernel(h_ref, a_ref, c_ref, wmo_ref, g_ref, wq_ref, kv_ref, wo_ref, o_ref):
    aw = ATTN_WIDTH
    scale = X_HEAD_DIM ** -0.5
    h1 = h_ref[0] + _dot(a_ref[0], wmo_ref[0, 0:aw, :]) + _dot(c_ref[0], wmo_ref[0, aw:, :])
    xn = _rms(h1, g_ref[0], NORM_EPS).astype(BF16)
    q = (_dot(xn, wq_ref[0]) * scale).astype(BF16)
    heads = []
    for hh in range(X_HEADS):
        lo, hi = hh * X_HEAD_DIM, (hh + 1) * X_HEAD_DIM
        s = _dot_nt(q[:, lo:hi], kv_ref[0, 0, :, lo:hi])
        p = jnp.exp(s - jnp.max(s, axis=-1, keepdims=True))
        l = jnp.sum(p, axis=-1, keepdims=True)
        o = _dot(p.astype(BF16), kv_ref[0, 0, :, D_MODEL + lo:D_MODEL + hi]) / l
        heads.append(o.astype(BF16))
    o_all = jnp.concatenate(heads, axis=-1)
    o_ref[0] = h1 + _dot(o_all, wo_ref[0])


def _mix_xattn(h, attn, conv, w_mo, g, w_xq, kv, w_xo, layer, w_layer, casts):
    b, s, d = h.shape
    tm = ROW_TILE
    n_j = s // tm
    cast_in, cast_out, cast_shapes = _cast_plan(casts, b * n_j, lambda i, j: i * n_j + j)
    in_specs = [
        pl.BlockSpec((1, tm, d), lambda i, j: (i, j, 0)),
        pl.BlockSpec((1, tm, ATTN_WIDTH), lambda i, j: (i, j, 0)),
        pl.BlockSpec((1, tm, CONV_WIDTH_CH), lambda i, j: (i, j, 0)),
        _layer_resident((d, d), w_layer),
        _layer_resident((1, d), layer),
        _layer_resident((d, d), w_layer),
        pl.BlockSpec((1, 1, MEM_LEN, 2 * d), lambda i, j: (layer, i, 0, 0)),
        _layer_resident((d, d), w_layer),
    ]
    return pl.pallas_call(
        _with_casts(_mix_xattn_kernel, len(in_specs), 1, len(casts)),
        grid=(b, n_j),
        in_specs=in_specs + cast_in,
        out_specs=[pl.BlockSpec((1, tm, d), lambda i, j: (i, j, 0))] + cast_out,
        out_shape=[jax.ShapeDtypeStruct((b, s, d), F32)] + cast_shapes,
        compiler_params=pltpu.CompilerParams(
            dimension_semantics=("arbitrary", "arbitrary"), vmem_limit_bytes=VMEM_LIMIT),
        name="mix_xattn",
    )(h, attn, conv, w_mo, g, w_xq, kv, w_xo, *(w for w, _ in casts))


def _ffn_kernel(h_ref, g_ref, w1_ref, w2_ref, fg_ref, o_ref, acc_ref, *, final_norm):
    h = h_ref[...]
    xn = _rms(h, g_ref[0], NORM_EPS).astype(BF16)
    for c in range(D_FF // FF_CHUNK):
        lo, hi = c * FF_CHUNK, (c + 1) * FF_CHUNK
        f = jnp.square(jnp.maximum(_dot(xn, w1_ref[0, :, lo:hi]), 0.0)).astype(BF16)
        part = _dot(f, w2_ref[0, lo:hi, :])
        if c == 0:
            acc_ref[...] = part
        else:
            acc_ref[...] += part
    out = h + acc_ref[...]
    if final_norm:
        out = _rms(out, fg_ref[...], NORM_EPS)
    o_ref[...] = out


def _ffn(h2d, g, w1, w2, final_g, final_norm, layer, w_layer):
    n, d = h2d.shape
    tm = ROW_TILE
    return pl.pallas_call(
        functools.partial(_ffn_kernel, final_norm=final_norm),
        grid=(n // tm,),
        in_specs=[
            pl.BlockSpec((tm, d), lambda i: (i, 0)),
            _layer_resident((1, d), layer),
            _layer_resident((d, D_FF), w_layer),
            _layer_resident((D_FF, d), w_layer),
            _resident((1, d)),
        ],
        out_specs=pl.BlockSpec((tm, d), lambda i: (i, 0)),
        out_shape=jax.ShapeDtypeStruct((n, d), F32),
        scratch_shapes=[pltpu.VMEM((tm, d), F32)],
        compiler_params=pltpu.CompilerParams(
            dimension_semantics=("arbitrary",), vmem_limit_bytes=VMEM_LIMIT),
        name="ffn",
    )(h2d, g, w1, w2, final_g)


def _mem_kv_kernel(m_ref, g_ref, w_ref, o_ref):
    mn = _rms(m_ref[...], g_ref[...], NORM_EPS).astype(BF16)
    o_ref[0] = _dot(mn, w_ref[0]).astype(BF16)


def _mem_kv(mem2d, g, w_xkv, casts):
    n, d = mem2d.shape
    depth, _, n_cols = w_xkv.shape
    tm = ROW_TILE
    n_i = n // tm
    cast_in, cast_out, cast_shapes = _cast_plan(casts, depth * n_i, lambda l, i: l * n_i + i)
    in_specs = [
        pl.BlockSpec((tm, d), lambda l, i: (i, 0)),
        pl.BlockSpec((1, d), lambda l, i: (0, 0)),
        pl.BlockSpec((1, d, n_cols), lambda l, i: (l, 0, 0)),
    ]
    return pl.pallas_call(
        _with_casts(_mem_kv_kernel, len(in_specs), 1, len(casts)),
        grid=(depth, n_i),
        in_specs=in_specs + cast_in,
        out_specs=[pl.BlockSpec((1, tm, n_cols), lambda l, i: (l, i, 0))] + cast_out,
        out_shape=[jax.ShapeDtypeStruct((depth, n, n_cols), BF16)] + cast_shapes,
        compiler_params=pltpu.CompilerParams(
            dimension_semantics=("arbitrary", "arbitrary"), vmem_limit_bytes=VMEM_LIMIT),
        name="mem_kv",
    )(mem2d, g, w_xkv, *(w for w, _ in casts))


def kernel(x, mem, positions, norm_mix_g, w_in, lam_q1, lam_k1, lam_q2, lam_k2, subln_g, conv_w,
           w_mix_out, norm_x_g, mem_norm_g, w_xq, w_xkv, w_xo, norm_ffn_g, w_ff1, w_ff2, final_g):
    b, s, d = x.shape
    aw = ATTN_WIDTH
    pos3 = positions.reshape(b, 1, s)
    inv_freq = (ROPE_THETA ** (-jnp.arange(0, ROT_DIM, 2, dtype=F32) / ROT_DIM)).reshape(ROT_DIM // 2, 1)
    w_qv = jnp.concatenate([w_in[:, :, 0:aw], w_in[:, :, 2 * aw:3 * aw]], axis=2)
    w_qvt_b = jnp.swapaxes(lax.optimization_barrier(w_qv), 1, 2).astype(BF16)
    g_mix, g_x, g_ffn = (g.reshape(DEPTH, 1, d) for g in (norm_mix_g, norm_x_g, norm_ffn_g))
    lam_rows = jnp.stack([lam_q1, lam_k1, lam_q2, lam_k2], axis=1).astype(F32)
    subln_col = subln_g.reshape(DEPTH, HEAD_W, 1)

    kv, w_in_b = _mem_kv(mem.reshape(b * MEM_LEN, d), mem_norm_g.reshape(1, d), w_xkv.astype(BF16),
                         casts=[(w_in, 0)])
    kv = kv.reshape(DEPTH, b, MEM_LEN, 2 * d)
    h = x
    for l in range(DEPTH):
        lambda_init = 0.8 - 0.6 * math.exp(-0.3 * l)
        qt, k, vt, conv, w_mo_b, w_xq_b, w_xo_b = _in_proj(
            h, pos3, inv_freq, g_mix, w_in_b, 0, w_qvt_b, conv_w, l,
            casts=[(w_mix_out, l), (w_xq, l), (w_xo, l)])
        attn = _diff_attn(lam_rows, qt, k, vt, subln_col, lambda_init, l)
        next_w_in = [(w_in, l + 1)] if l + 1 < DEPTH else []
        h, w_ff1_b, w_ff2_b, *w_in_next = _mix_xattn(
            h, attn, conv, w_mo_b, g_x, w_xq_b, kv, w_xo_b, l, 0,
            casts=[(w_ff1, l), (w_ff2, l)] + next_w_in)
        if w_in_next:
            w_in_b = w_in_next[0]
        h = _ffn(h.reshape(b * s, d), g_ffn, w_ff1_b, w_ff2_b, final_g.reshape(1, d),
                 l == DEPTH - 1, l, 0).reshape(b, s, d)
    return h
```

```python
import functools
import math

import jax
import jax.numpy as jnp
from jax import lax
from jax.experimental import pallas as pl
from jax.experimental.pallas import tpu as pltpu

D_MODEL = 1024
DEPTH = 2
MEM_LEN = 256
ATTN_WIDTH = D_MODEL // 2
CONV_WIDTH_CH = D_MODEL - ATTN_WIDTH
DIFF_HEAD_DIM = 64
DIFF_HEADS = ATTN_WIDTH // (2 * DIFF_HEAD_DIM)
HEAD_W = 2 * DIFF_HEAD_DIM
CONV_K = 3
ROT_DIM = DIFF_HEAD_DIM // 4
ROPE_THETA = 500000.0
X_HEADS = 4
X_HEAD_DIM = D_MODEL // X_HEADS
D_FF = 4 * D_MODEL
NEG_INF = -1e30
NORM_EPS = 1e-6
SUBLN_EPS = 1e-5

LANES = 128
SUBLANES = 8
VMEM_LIMIT = 56 * 1024 * 1024
ATTN_VMEM_LIMIT = 32 * 1024 * 1024

ROW_TILE = 1024
Q_TILE = 512
ATTN_HEADS_PER_STEP = 2
SUM_ROWS = 16
FF_CHUNK = 1024

BF16 = jnp.bfloat16
F32 = jnp.float32


def _rms(x, g, eps):
    return x * lax.rsqrt(jnp.mean(x * x, axis=-1, keepdims=True) + eps) * g


def _dot(a, b):
    return jnp.dot(a, b, preferred_element_type=F32)


def _dot_nt(a, b):
    return lax.dot_general(a, b, (((1,), (1,)), ((), ())), preferred_element_type=F32)


def _resident(shape):
    zeros = (0,) * len(shape)
    return pl.BlockSpec(shape, lambda *_: zeros, pipeline_mode=pl.Buffered(1))


def _layer_resident(shape, layer):
    index = (layer,) + (0,) * len(shape)
    return pl.BlockSpec((1,) + tuple(shape), lambda *_: index, pipeline_mode=pl.Buffered(1))


def _cast_plan(casts, n_steps, step_of):
    in_specs, out_specs, out_shapes = [], [], []
    for w, layer in casts:
        r, c = w.shape[1:]
        rows = r // n_steps
        in_specs.append(pl.BlockSpec((1, rows, c), lambda *g, layer=layer: (layer, step_of(*g), 0)))
        out_specs.append(pl.BlockSpec((1, rows, c), lambda *g: (0, step_of(*g), 0)))
        out_shapes.append(jax.ShapeDtypeStruct((1, r, c), BF16))
    return in_specs, out_specs, out_shapes


def _with_casts(body, n_in, n_out, n_cast):
    def kernel(*refs):
        ins, cast_in = refs[:n_in], refs[n_in:n_in + n_cast]
        outs = refs[n_in + n_cast:n_in + n_cast + n_out]
        cast_out = refs[n_in + n_cast + n_out:n_in + 2 * n_cast + n_out]
        scratch = refs[n_in + 2 * n_cast + n_out:]
        for src, dst in zip(cast_in, cast_out):
            dst[...] = src[...].astype(BF16)
        body(*ins, *outs, *scratch)
    return kernel


def _in_proj_kernel(pos_ref, inv_freq_ref, x_ref, g_ref, w_ref, wvt_ref, cw_ref,
                    q_ref, k_ref, vt_ref, c_ref, carry_ref):
    tm = x_ref.shape[1]
    aw = ATTN_WIDTH

    @pl.when(pl.program_id(1) == 0)
    def _():
        carry_ref[...] = jnp.zeros_like(carry_ref)

    xn = _rms(x_ref[0], g_ref[0], NORM_EPS).astype(BF16)

    half = ROT_DIM // 2
    ang_t = inv_freq_ref[...] * pos_ref[0].astype(F32)
    cos_t, sin_t = jnp.cos(ang_t), jnp.sin(ang_t)
    rest = DIFF_HEAD_DIM - ROT_DIM
    ones, zeros = jnp.ones((rest, tm), F32), jnp.zeros((rest, tm), F32)
    cos = jnp.concatenate([cos_t, cos_t, ones] * 2, axis=0).T
    sin = jnp.concatenate([-sin_t, sin_t, zeros] * 2, axis=0).T
    lane = lax.broadcasted_iota(jnp.int32, (tm, HEAD_W), 1)
    pairs_up = (lane & (DIFF_HEAD_DIM - 1)) < half

    def rotate(t):
        up = pltpu.roll(t, LANES - half, axis=1)
        dn = pltpu.roll(t, half, axis=1)
        return t * cos + jnp.where(pairs_up, up, dn) * sin

    scale = math.log2(math.e) * DIFF_HEAD_DIM ** -0.5
    q = _dot(xn, w_ref[0, :, 0:aw])
    k = _dot(xn, w_ref[0, :, aw:2 * aw])
    for h in range(DIFF_HEADS):
        lo, hi = h * HEAD_W, (h + 1) * HEAD_W
        q_ref[0, :, lo:hi] = (rotate(q[:, lo:hi]) * scale).astype(BF16)
        k_ref[0, :, lo:hi] = rotate(k[:, lo:hi]).astype(BF16)

    cwid = CONV_WIDTH_CH
    c_gate = _dot(xn, w_ref[0, :, 3 * aw + cwid:3 * aw + 2 * cwid])
    hc = _dot(xn, w_ref[0, :, 3 * aw + 2 * cwid:3 * aw + 3 * cwid])
    b_gate = _dot(xn, w_ref[0, :, 3 * aw:3 * aw + cwid])
    u = c_gate * hc
    prev = carry_ref[...]
    row = lax.broadcasted_iota(jnp.int32, u.shape, 0)
    last = prev[SUBLANES - 1:SUBLANES, :]
    last2 = prev[SUBLANES - 2:SUBLANES - 1, :]
    u1 = jnp.where(row == 0, last, pltpu.roll(u, 1, axis=0))
    u2 = jnp.where(row == 0, last2, jnp.where(row == 1, last, pltpu.roll(u, 2, axis=0)))
    cw = cw_ref[0]
    y = cw[0:1, :] * u2 + cw[1:2, :] * u1 + cw[2:3, :] * u
    c_ref[0] = (b_gate * y).astype(BF16)
    carry_ref[...] = u[tm - SUBLANES:tm, :]

    vt = _dot_nt(wvt_ref[0], xn).astype(BF16)
    tk = vt_ref.shape[3]
    for i in range(tm // tk):
        vt_ref[0, i] = vt[:, i * tk:(i + 1) * tk]


def _in_proj(h, pos3, inv_freq, g, w_in, w_in_layer, w_vt, conv_w, layer, casts):
    b, s, d = h.shape
    tm = ROW_TILE
    n_cols = w_in.shape[2]
    act = jax.ShapeDtypeStruct((b, s, ATTN_WIDTH), BF16)
    act_spec = pl.BlockSpec((1, tm, ATTN_WIDTH), lambda i, j: (i, j, 0))
    tk = Q_TILE
    vt = jax.ShapeDtypeStruct((b, s // tk, ATTN_WIDTH, tk), BF16)
    vt_spec = pl.BlockSpec((1, tm // tk, ATTN_WIDTH, tk), lambda i, j: (i, j, 0, 0))
    n_j = s // tm
    cast_in, cast_out, cast_shapes = _cast_plan(casts, b * n_j, lambda i, j: i * n_j + j)
    in_specs = [
        pl.BlockSpec((1, 1, tm), lambda i, j: (i, 0, j)),
        _resident((ROT_DIM // 2, 1)),
        pl.BlockSpec((1, tm, d), lambda i, j: (i, j, 0)),
        _layer_resident((1, d), layer),
        _layer_resident((d, n_cols), w_in_layer),
        _layer_resident((ATTN_WIDTH, d), layer),
        _layer_resident((CONV_K, CONV_WIDTH_CH), layer),
    ]
    return pl.pallas_call(
        _with_casts(_in_proj_kernel, len(in_specs), 4, len(casts)),
        grid=(b, n_j),
        in_specs=in_specs + cast_in,
        out_specs=[act_spec, act_spec, vt_spec, act_spec] + cast_out,
        out_shape=[act, act, vt, act] + cast_shapes,
        scratch_shapes=[pltpu.VMEM((SUBLANES, CONV_WIDTH_CH), F32)],
        compiler_params=pltpu.CompilerParams(
            dimension_semantics=("arbitrary", "arbitrary"), vmem_limit_bytes=VMEM_LIMIT),
        name="in_proj",
    )(pos3, inv_freq, h, g, w_in, w_vt, conv_w, *(w for w, _ in casts))


def _diff_attn_kernel(lam_ref, q_ref, k_ref, vt_ref, g_ref, o_ref, *, out_scale, lambda_init):
    tq = Q_TILE
    tk = vt_ref.shape[3]
    n_q = q_ref.shape[1] // tq
    heads = range(q_ref.shape[2] // HEAD_W)
    lam_rows = lam_ref[0]
    lam = (jnp.exp(jnp.sum(lam_rows[0:1, :] * lam_rows[1:2, :], axis=-1, keepdims=True))
           - jnp.exp(jnp.sum(lam_rows[2:3, :] * lam_rows[3:4, :], axis=-1, keepdims=True))
           + lambda_init)
    g_col = g_ref[0]
    ones_rows = jnp.ones((SUM_ROWS, tk), BF16)
    hq = tq // 2

    def pieces(qi, j):
        if (j + 1) * tk >= (qi + 1) * tq:
            return [(0, hq, hq, True), (hq, tq - hq, tk, True)]
        return [(0, tq, tk, False)]

    def scores(hd, qi, j):
        lanes = slice(hd * HEAD_W, (hd + 1) * HEAD_W)
        out = []
        for c0, nc, nk, _ in pieces(qi, j):
            qh = q_ref[0, qi * tq + c0:qi * tq + c0 + nc, lanes]
            first = lax.broadcasted_iota(jnp.int32, qh.shape, 1) < DIFF_HEAD_DIM
            zero = jnp.zeros_like(qh)
            kj = k_ref[0, j * tk:j * tk + nk, lanes]
            out.append((_dot_nt(kj, jnp.where(first, qh, zero)), _dot_nt(kj, jnp.where(first, zero, qh))))
        return out

    def finalize(hd, row0, state):
        (_, l0, a0), (_, l1, a1) = state
        ot = a0 / l0 - lam * (a1 / l1)
        ms = jnp.mean(ot * ot, axis=0, keepdims=True)
        ot = ot * lax.rsqrt(ms + SUBLN_EPS) * g_col * out_scale
        o_ref[0, row0:row0 + ot.shape[1], hd * HEAD_W:(hd + 1) * HEAD_W] = ot.T.astype(o_ref.dtype)

    def softmax_pv(hd, qi, j, s_cur, state):
        vt_aug = jnp.concatenate([vt_ref[0, j, hd * HEAD_W:(hd + 1) * HEAD_W, :], ones_rows], axis=0)
        new_state = []
        for (c0, nc, nk, masked), s_piece in zip(pieces(qi, j), s_cur):
            piece_state = []
            for c in range(2):
                s = s_piece[c]
                if masked:
                    key = lax.broadcasted_iota(jnp.int32, (nk, nc), 0)
                    qry = lax.broadcasted_iota(jnp.int32, (nk, nc), 1) + c0
                    s = jnp.where(key <= qry, s, NEG_INF)
                m_new = jnp.max(s, axis=0, keepdims=True)
                if j > 0:
                    m_old, l_old, acc_old = (a[:, c0:c0 + nc] for a in state[c])
                    m_new = jnp.maximum(m_old, m_new)
                    alpha = jnp.exp2(m_old - m_new)
                pv = _dot(vt_aug[:, 0:nk], jnp.exp2(s - m_new).astype(BF16))
                acc, l = pv[:HEAD_W], pv[HEAD_W:HEAD_W + 1]
                if j > 0:
                    acc, l = alpha * acc_old + acc, alpha * l_old + l
                piece_state.append((m_new, l, acc))
            if masked:
                finalize(hd, qi * tq + c0, piece_state)
            else:
                new_state = piece_state
        return new_state

    pairs = [(qi, j) for qi in range(n_q) for j in range((qi * tq) // tk + 1)]
    s_next = [scores(hd, *pairs[0]) for hd in heads]
    state = [None for _ in heads]
    for t, (qi, j) in enumerate(pairs):
        s_cur = s_next
        if t + 1 < len(pairs):
            s_next = [scores(hd, *pairs[t + 1]) for hd in heads]
        state = [softmax_pv(hd, qi, j, s_cur[hd], state[hd]) for hd in heads]


def _diff_attn(lam_rows, q, k, vt, subln_g_col, lambda_init, layer):
    b, s, w = q.shape
    nk, tk = vt.shape[1], vt.shape[3]
    assert Q_TILE == tk, "diagonal tiles are assumed square"
    gw = ATTN_HEADS_PER_STEP * HEAD_W
    kern = functools.partial(_diff_attn_kernel, out_scale=1.0 - lambda_init, lambda_init=lambda_init)
    return pl.pallas_call(
        kern,
        grid=(b, DIFF_HEADS // ATTN_HEADS_PER_STEP),
        in_specs=[
            _layer_resident((4, DIFF_HEAD_DIM), layer),
            pl.BlockSpec((1, s, gw), lambda i, h: (i, 0, h)),
            pl.BlockSpec((1, s, gw), lambda i, h: (i, 0, h)),
            pl.BlockSpec((1, nk, gw, tk), lambda i, h: (i, 0, h, 0)),
            _layer_resident((HEAD_W, 1), layer),
        ],
        out_specs=pl.BlockSpec((1, s, gw), lambda i, h: (i, 0, h)),
        out_shape=jax.ShapeDtypeStruct((b, s, w), BF16),
        compiler_params=pltpu.CompilerParams(
            dimension_semantics=("arbitrary", "arbitrary"), vmem_limit_bytes=ATTN_VMEM_LIMIT),
        name="diff_attn",
    )(lam_rows, q, k, vt, subln_g_col)


def _mix_xattn_kernel(h_ref, a_ref, c_ref, wmo_ref, g_ref, wq_ref, kv_ref, wo_ref, o_ref):
    aw = ATTN_WIDTH
    scale = X_HEAD_DIM ** -0.5
    h1 = h_ref[0] + _dot(a_ref[0], wmo_ref[0, 0:aw, :]) + _dot(c_ref[0], wmo_ref[0, aw:, :])
    xn = _rms(h1, g_ref[0], NORM_EPS).astype(BF16)
    q = (_dot(xn, wq_ref[0]) * scale).astype(BF16)
    heads = []
    for hh in range(X_HEADS):
        lo, hi = hh * X_HEAD_DIM, (hh + 1) * X_HEAD_DIM
        s = _dot_nt(q[:, lo:hi], kv_ref[0, 0, :, lo:hi])
        p = jnp.exp(s - jnp.max(s, axis=-1, keepdims=True))
        l = jnp.sum(p, axis=-1, keepdims=True)
        o = _dot(p.astype(BF16), kv_ref[0, 0, :, D_MODEL + lo:D_MODEL + hi]) / l
        heads.append(o.astype(BF16))
    o_all = jnp.concatenate(heads, axis=-1)
    o_ref[0] = h1 + _dot(o_all, wo_ref[0])


def _mix_xattn(h, attn, conv, w_mo, g, w_xq, kv, w_xo, layer, w_layer, casts):
    b, s, d = h.shape
    tm = ROW_TILE
    n_j = s // tm
    cast_in, cast_out, cast_shapes = _cast_plan(casts, b * n_j, lambda i, j: i * n_j + j)
    in_specs = [
        pl.BlockSpec((1, tm, d), lambda i, j: (i, j, 0)),
        pl.BlockSpec((1, tm, ATTN_WIDTH), lambda i, j: (i, j, 0)),
        pl.BlockSpec((1, tm, CONV_WIDTH_CH), lambda i, j: (i, j, 0)),
        _layer_resident((d, d), w_layer),
        _layer_resident((1, d), layer),
        _layer_resident((d, d), w_layer),
        pl.BlockSpec((1, 1, MEM_LEN, 2 * d), lambda i, j: (layer, i, 0, 0)),
        _layer_resident((d, d), w_layer),
    ]
    return pl.pallas_call(
        _with_casts(_mix_xattn_kernel, len(in_specs), 1, len(casts)),
        grid=(b, n_j),
        in_specs=in_specs + cast_in,
        out_specs=[pl.BlockSpec((1, tm, d), lambda i, j: (i, j, 0))] + cast_out,
        out_shape=[jax.ShapeDtypeStruct((b, s, d), F32)] + cast_shapes,
        compiler_params=pltpu.CompilerParams(
            dimension_semantics=("arbitrary", "arbitrary"), vmem_limit_bytes=VMEM_LIMIT),
        name="mix_xattn",
    )(h, attn, conv, w_mo, g, w_xq, kv, w_xo, *(w for w, _ in casts))


def _ffn_kernel(h_ref, g_ref, w1_ref, w2_ref, fg_ref, o_ref, acc_ref, *, final_norm):
    h = h_ref[...]
    xn = _rms(h, g_ref[0], NORM_EPS).astype(BF16)
    for c in range(D_FF // FF_CHUNK):
        lo, hi = c * FF_CHUNK, (c + 1) * FF_CHUNK
        f = jnp.square(jnp.maximum(_dot(xn, w1_ref[0, :, lo:hi]), 0.0)).astype(BF16)
        part = _dot(f, w2_ref[0, lo:hi, :])
        if c == 0:
            acc_ref[...] = part
        else:
            acc_ref[...] += part
    out = h + acc_ref[...]
    if final_norm:
        out = _rms(out, fg_ref[...], NORM_EPS)
    o_ref[...] = out


def _ffn(h2d, g, w1, w2, final_g, final_norm, layer, w_layer):
    n, d = h2d.shape
    tm = ROW_TILE
    return pl.pallas_call(
        functools.partial(_ffn_kernel, final_norm=final_norm),
        grid=(n // tm,),
        in_specs=[
            pl.BlockSpec((tm, d), lambda i: (i, 0)),
            _layer_resident((1, d), layer),
            _layer_resident((d, D_FF), w_layer),
            _layer_resident((D_FF, d), w_layer),
            _resident((1, d)),
        ],
        out_specs=pl.BlockSpec((tm, d), lambda i: (i, 0)),
        out_shape=jax.ShapeDtypeStruct((n, d), F32),
        scratch_shapes=[pltpu.VMEM((tm, d), F32)],
        compiler_params=pltpu.CompilerParams(
            dimension_semantics=("arbitrary",), vmem_limit_bytes=VMEM_LIMIT),
        name="ffn",
    )(h2d, g, w1, w2, final_g)


def _mem_kv_kernel(m_ref, g_ref, w_ref, o_ref):
    mn = _rms(m_ref[...], g_ref[...], NORM_EPS).astype(BF16)
    o_ref[0] = _dot(mn, w_ref[0]).astype(BF16)


def _mem_kv(mem2d, g, w_xkv, casts):
    n, d = mem2d.shape
    depth, _, n_cols = w_xkv.shape
    tm = ROW_TILE
    n_i = n // tm
    cast_in, cast_out, cast_shapes = _cast_plan(casts, depth * n_i, lambda l, i: l * n_i + i)
    in_specs = [
        pl.BlockSpec((tm, d), lambda l, i: (i, 0)),
        pl.BlockSpec((1, d), lambda l, i: (0, 0)),
        pl.BlockSpec((1, d, n_cols), lambda l, i: (l, 0, 0)),
    ]
    return pl.pallas_call(
        _with_casts(_mem_kv_kernel, len(in_specs), 1, len(casts)),
        grid=(depth, n_i),
        in_specs=in_specs + cast_in,
        out_specs=[pl.BlockSpec((1, tm, n_cols), lambda l, i: (l, i, 0))] + cast_out,
        out_shape=[jax.ShapeDtypeStruct((depth, n, n_cols), BF16)] + cast_shapes,
        compiler_params=pltpu.CompilerParams(
            dimension_semantics=("arbitrary", "arbitrary"), vmem_limit_bytes=VMEM_LIMIT),
        name="mem_kv",
    )(mem2d, g, w_xkv, *(w for w, _ in casts))


def kernel(x, mem, positions, norm_mix_g, w_in, lam_q1, lam_k1, lam_q2, lam_k2, subln_g, conv_w,
           w_mix_out, norm_x_g, mem_norm_g, w_xq, w_xkv, w_xo, norm_ffn_g, w_ff1, w_ff2, final_g):
    b, s, d = x.shape
    aw = ATTN_WIDTH
    pos3 = positions.reshape(b, 1, s)
    inv_freq = (ROPE_THETA ** (-jnp.arange(0, ROT_DIM, 2, dtype=F32) / ROT_DIM)).reshape(ROT_DIM // 2, 1)
    w_vt_b = jnp.swapaxes(lax.optimization_barrier(w_in[:, :, 2 * aw:3 * aw]), 1, 2).astype(BF16)
    g_mix, g_x, g_ffn = (g.reshape(DEPTH, 1, d) for g in (norm_mix_g, norm_x_g, norm_ffn_g))
    lam_rows = jnp.stack([lam_q1, lam_k1, lam_q2, lam_k2], axis=1).astype(F32)
    subln_col = subln_g.reshape(DEPTH, HEAD_W, 1)

    kv, w_in_b = _mem_kv(mem.reshape(b * MEM_LEN, d), mem_norm_g.reshape(1, d), w_xkv.astype(BF16),
                         casts=[(w_in, 0)])
    kv = kv.reshape(DEPTH, b, MEM_LEN, 2 * d)
    h = x
    for l in range(DEPTH):
        lambda_init = 0.8 - 0.6 * math.exp(-0.3 * l)
        q, k, vt, conv, w_mo_b, w_xq_b, w_xo_b = _in_proj(
            h, pos3, inv_freq, g_mix, w_in_b, 0, w_vt_b, conv_w, l,
            casts=[(w_mix_out, l), (w_xq, l), (w_xo, l)])
        attn = _diff_attn(lam_rows, q, k, vt, subln_col, lambda_init, l)
        next_w_in = [(w_in, l + 1)] if l + 1 < DEPTH else []
        h, w_ff1_b, w_ff2_b, *w_in_next = _mix_xattn(
            h, attn, conv, w_mo_b, g_x, w_xq_b, kv, w_xo_b, l, 0,
            casts=[(w_ff1, l), (w_ff2, l)] + next_w_in)
        if w_in_next:
            w_in_b = w_in_next[0]
        h = _ffn(h.reshape(b * s, d), g_ffn, w_ff1_b, w_ff2_b, final_g.reshape(1, d),
                 l == DEPTH - 1, l, 0).reshape(b, s, d)
    return h
```

```python
import functools
import math

import jax
import jax.numpy as jnp
from jax import lax
from jax.experimental import pallas as pl
from jax.experimental.pallas import tpu as pltpu

D_MODEL = 1024
DEPTH = 2
MEM_LEN = 256
ATTN_WIDTH = D_MODEL // 2
CONV_WIDTH_CH = D_MODEL - ATTN_WIDTH
DIFF_HEAD_DIM = 64
DIFF_HEADS = ATTN_WIDTH // (2 * DIFF_HEAD_DIM)
HEAD_W = 2 * DIFF_HEAD_DIM
CONV_K = 3
ROT_DIM = DIFF_HEAD_DIM // 4
ROPE_THETA = 500000.0
X_HEADS = 4
X_HEAD_DIM = D_MODEL // X_HEADS
D_FF = 4 * D_MODEL
NEG_INF = -1e30
NORM_EPS = 1e-6
SUBLN_EPS = 1e-5

LANES = 128
SUBLANES = 8
VMEM_LIMIT = 56 * 1024 * 1024

ROW_TILE = 1024
Q_TILE = 512
ATTN_HEADS_PER_STEP = 2
SUM_ROWS = 16
FF_CHUNK = 1024

BF16 = jnp.bfloat16
F32 = jnp.float32


def _rms(x, g, eps):
    return x * lax.rsqrt(jnp.mean(x * x, axis=-1, keepdims=True) + eps) * g


def _dot(a, b):
    return jnp.dot(a, b, preferred_element_type=F32)


def _dot_nt(a, b):
    return lax.dot_general(a, b, (((1,), (1,)), ((), ())), preferred_element_type=F32)


def _resident(shape):
    zeros = (0,) * len(shape)
    return pl.BlockSpec(shape, lambda *_: zeros, pipeline_mode=pl.Buffered(1))


def _layer_resident(shape, layer):
    index = (layer,) + (0,) * len(shape)
    return pl.BlockSpec((1,) + tuple(shape), lambda *_: index, pipeline_mode=pl.Buffered(1))


def _cast_plan(casts, n_steps, step_of):
    in_specs, out_specs, out_shapes = [], [], []
    for w, layer in casts:
        r, c = w.shape[1:]
        rows = r // n_steps
        in_specs.append(pl.BlockSpec((1, rows, c), lambda *g, layer=layer: (layer, step_of(*g), 0)))
        out_specs.append(pl.BlockSpec((1, rows, c), lambda *g: (0, step_of(*g), 0)))
        out_shapes.append(jax.ShapeDtypeStruct((1, r, c), BF16))
    return in_specs, out_specs, out_shapes


def _with_casts(body, n_in, n_out, n_cast):
    def kernel(*refs):
        ins, cast_in = refs[:n_in], refs[n_in:n_in + n_cast]
        outs = refs[n_in + n_cast:n_in + n_cast + n_out]
        cast_out = refs[n_in + n_cast + n_out:n_in + 2 * n_cast + n_out]
        scratch = refs[n_in + 2 * n_cast + n_out:]
        for src, dst in zip(cast_in, cast_out):
            dst[...] = src[...].astype(BF16)
        body(*ins, *outs, *scratch)
    return kernel


def _in_proj_kernel(pos_ref, inv_freq_ref, x_ref, g_ref, w_ref, wvt_ref, cw_ref,
                    q_ref, k_ref, vt_ref, c_ref, carry_ref):
    tm = x_ref.shape[1]
    aw = ATTN_WIDTH

    @pl.when(pl.program_id(1) == 0)
    def _():
        carry_ref[...] = jnp.zeros_like(carry_ref)

    xn = _rms(x_ref[0], g_ref[0], NORM_EPS).astype(BF16)

    half = ROT_DIM // 2
    ang_t = inv_freq_ref[...] * pos_ref[0].astype(F32)
    cos_t, sin_t = jnp.cos(ang_t), jnp.sin(ang_t)
    rest = DIFF_HEAD_DIM - ROT_DIM
    ones, zeros = jnp.ones((rest, tm), F32), jnp.zeros((rest, tm), F32)
    cos = jnp.concatenate([cos_t, cos_t, ones] * 2, axis=0).T
    sin = jnp.concatenate([-sin_t, sin_t, zeros] * 2, axis=0).T
    lane = lax.broadcasted_iota(jnp.int32, (tm, HEAD_W), 1)
    pairs_up = (lane & (DIFF_HEAD_DIM - 1)) < half

    def rotate(t):
        up = pltpu.roll(t, LANES - half, axis=1)
        dn = pltpu.roll(t, half, axis=1)
        return t * cos + jnp.where(pairs_up, up, dn) * sin

    scale = math.log2(math.e) * DIFF_HEAD_DIM ** -0.5
    q = _dot(xn, w_ref[0, :, 0:aw])
    k = _dot(xn, w_ref[0, :, aw:2 * aw])
    for h in range(DIFF_HEADS):
        lo, hi = h * HEAD_W, (h + 1) * HEAD_W
        q_ref[0, :, lo:hi] = (rotate(q[:, lo:hi]) * scale).astype(BF16)
        k_ref[0, :, lo:hi] = rotate(k[:, lo:hi]).astype(BF16)

    cwid = CONV_WIDTH_CH
    c_gate = _dot(xn, w_ref[0, :, 3 * aw + cwid:3 * aw + 2 * cwid])
    hc = _dot(xn, w_ref[0, :, 3 * aw + 2 * cwid:3 * aw + 3 * cwid])
    b_gate = _dot(xn, w_ref[0, :, 3 * aw:3 * aw + cwid])
    u = c_gate * hc
    prev = carry_ref[...]
    row = lax.broadcasted_iota(jnp.int32, u.shape, 0)
    last = prev[SUBLANES - 1:SUBLANES, :]
    last2 = prev[SUBLANES - 2:SUBLANES - 1, :]
    u1 = jnp.where(row == 0, last, pltpu.roll(u, 1, axis=0))
    u2 = jnp.where(row == 0, last2, jnp.where(row == 1, last, pltpu.roll(u, 2, axis=0)))
    cw = cw_ref[0]
    y = cw[0:1, :] * u2 + cw[1:2, :] * u1 + cw[2:3, :] * u
    c_ref[0] = (b_gate * y).astype(BF16)
    carry_ref[...] = u[tm - SUBLANES:tm, :]

    vt = _dot_nt(wvt_ref[0], xn).astype(BF16)
    tk = vt_ref.shape[3]
    for i in range(tm // tk):
        vt_ref[0, i] = vt[:, i * tk:(i + 1) * tk]


def _in_proj(h, pos3, inv_freq, g, w_in, w_in_layer, w_vt, conv_w, layer, casts):
    b, s, d = h.shape
    tm = ROW_TILE
    n_cols = w_in.shape[2]
    act = jax.ShapeDtypeStruct((b, s, ATTN_WIDTH), BF16)
    act_spec = pl.BlockSpec((1, tm, ATTN_WIDTH), lambda i, j: (i, j, 0))
    tk = Q_TILE
    vt = jax.ShapeDtypeStruct((b, s // tk, ATTN_WIDTH, tk), BF16)
    vt_spec = pl.BlockSpec((1, tm // tk, ATTN_WIDTH, tk), lambda i, j: (i, j, 0, 0))
    n_j = s // tm
    cast_in, cast_out, cast_shapes = _cast_plan(casts, b * n_j, lambda i, j: i * n_j + j)
    in_specs = [
        pl.BlockSpec((1, 1, tm), lambda i, j: (i, 0, j)),
        _resident((ROT_DIM // 2, 1)),
        pl.BlockSpec((1, tm, d), lambda i, j: (i, j, 0)),
        _layer_resident((1, d), layer),
        _layer_resident((d, n_cols), w_in_layer),
        _layer_resident((ATTN_WIDTH, d), layer),
        _layer_resident((CONV_K, CONV_WIDTH_CH), layer),
    ]
    return pl.pallas_call(
        _with_casts(_in_proj_kernel, len(in_specs), 4, len(casts)),
        grid=(b, n_j),
        in_specs=in_specs + cast_in,
        out_specs=[act_spec, act_spec, vt_spec, act_spec] + cast_out,
        out_shape=[act, act, vt, act] + cast_shapes,
        scratch_shapes=[pltpu.VMEM((SUBLANES, CONV_WIDTH_CH), F32)],
        compiler_params=pltpu.CompilerParams(
            dimension_semantics=("arbitrary", "arbitrary"), vmem_limit_bytes=VMEM_LIMIT),
        name="in_proj",
    )(pos3, inv_freq, h, g, w_in, w_vt, conv_w, *(w for w, _ in casts))


def _diff_attn_kernel(lam_ref, q_ref, k_ref, vt_ref, g_ref, o_ref, *, out_scale, lambda_init):
    tq = Q_TILE
    tk = vt_ref.shape[3]
    n_q = q_ref.shape[1] // tq
    heads = range(q_ref.shape[2] // HEAD_W)
    lam_rows = lam_ref[0]
    lam = (jnp.exp(jnp.sum(lam_rows[0:1, :] * lam_rows[1:2, :], axis=-1, keepdims=True))
           - jnp.exp(jnp.sum(lam_rows[2:3, :] * lam_rows[3:4, :], axis=-1, keepdims=True))
           + lambda_init)
    g_col = g_ref[0]
    ones_rows = jnp.ones((SUM_ROWS, tk), BF16)
    hq = tq // 2

    def pieces(qi, j):
        if (j + 1) * tk >= (qi + 1) * tq:
            return [(0, hq, hq, True), (hq, tq - hq, tk, True)]
        return [(0, tq, tk, False)]

    def scores(hd, qi, j):
        lanes = slice(hd * HEAD_W, (hd + 1) * HEAD_W)
        out = []
        for c0, nc, nk, _ in pieces(qi, j):
            qh = q_ref[0, qi * tq + c0:qi * tq + c0 + nc, lanes]
            first = lax.broadcasted_iota(jnp.int32, qh.shape, 1) < DIFF_HEAD_DIM
            zero = jnp.zeros_like(qh)
            kj = k_ref[0, j * tk:j * tk + nk, lanes]
            out.append((_dot_nt(kj, jnp.where(first, qh, zero)), _dot_nt(kj, jnp.where(first, zero, qh))))
        return out

    def finalize(hd, row0, state):
        (_, l0, a0), (_, l1, a1) = state
        ot = a0 / l0 - lam * (a1 / l1)
        ms = jnp.mean(ot * ot, axis=0, keepdims=True)
        ot = ot * lax.rsqrt(ms + SUBLN_EPS) * g_col * out_scale
        o_ref[0, row0:row0 + ot.shape[1], hd * HEAD_W:(hd + 1) * HEAD_W] = ot.T.astype(o_ref.dtype)

    def softmax_pv(hd, qi, j, s_cur, state):
        vt_aug = jnp.concatenate([vt_ref[0, j, hd * HEAD_W:(hd + 1) * HEAD_W, :], ones_rows], axis=0)
        new_state = []
        for (c0, nc, nk, masked), s_piece in zip(pieces(qi, j), s_cur):
            piece_state = []
            for c in range(2):
                s = s_piece[c]
                if masked:
                    key = lax.broadcasted_iota(jnp.int32, (nk, nc), 0)
                    qry = lax.broadcasted_iota(jnp.int32, (nk, nc), 1) + c0
                    s = jnp.where(key <= qry, s, NEG_INF)
                m_new = jnp.max(s, axis=0, keepdims=True)
                if j > 0:
                    m_old, l_old, acc_old = (a[:, c0:c0 + nc] for a in state[c])
                    m_new = jnp.maximum(m_old, m_new)
                    alpha = jnp.exp2(m_old - m_new)
                pv = _dot(vt_aug[:, 0:nk], jnp.exp2(s - m_new).astype(BF16))
                acc, l = pv[:HEAD_W], pv[HEAD_W:HEAD_W + 1]
                if j > 0:
                    acc, l = alpha * acc_old + acc, alpha * l_old + l
                piece_state.append((m_new, l, acc))
            if masked:
                finalize(hd, qi * tq + c0, piece_state)
            else:
                new_state = piece_state
        return new_state

    pairs = [(qi, j) for qi in range(n_q) for j in range((qi * tq) // tk + 1)]
    s_next = [scores(hd, *pairs[0]) for hd in heads]
    state = [None for _ in heads]
    for t, (qi, j) in enumerate(pairs):
        s_cur = s_next
        if t + 1 < len(pairs):
            s_next = [scores(hd, *pairs[t + 1]) for hd in heads]
        state = [softmax_pv(hd, qi, j, s_cur[hd], state[hd]) for hd in heads]


def _diff_attn(lam_rows, q, k, vt, subln_g_col, lambda_init, layer):
    b, s, w = q.shape
    nk, tk = vt.shape[1], vt.shape[3]
    assert Q_TILE == tk, "diagonal tiles are assumed square"
    gw = ATTN_HEADS_PER_STEP * HEAD_W
    kern = functools.partial(_diff_attn_kernel, out_scale=1.0 - lambda_init, lambda_init=lambda_init)
    return pl.pallas_call(
        kern,
        grid=(b, DIFF_HEADS // ATTN_HEADS_PER_STEP),
        in_specs=[
            _layer_resident((4, DIFF_HEAD_DIM), layer),
            pl.BlockSpec((1, s, gw), lambda i, h: (i, 0, h)),
            pl.BlockSpec((1, s, gw), lambda i, h: (i, 0, h)),
            pl.BlockSpec((1, nk, gw, tk), lambda i, h: (i, 0, h, 0)),
            _layer_resident((HEAD_W, 1), layer),
        ],
        out_specs=pl.BlockSpec((1, s, gw), lambda i, h: (i, 0, h)),
        out_shape=jax.ShapeDtypeStruct((b, s, w), BF16),
        compiler_params=pltpu.CompilerParams(
            dimension_semantics=("arbitrary", "arbitrary"), vmem_limit_bytes=VMEM_LIMIT),
        name="diff_attn",
    )(lam_rows, q, k, vt, subln_g_col)


def _mix_xattn_kernel(h_ref, a_ref, c_ref, wmo_ref, g_ref, wq_ref, kv_ref, wo_ref, o_ref):
    aw = ATTN_WIDTH
    scale = X_HEAD_DIM ** -0.5
    h1 = h_ref[0] + _dot(a_ref[0], wmo_ref[0, 0:aw, :]) + _dot(c_ref[0], wmo_ref[0, aw:, :])
    xn = _rms(h1, g_ref[0], NORM_EPS).astype(BF16)
    q = (_dot(xn, wq_ref[0]) * scale).astype(BF16)
    heads = []
    for hh in range(X_HEADS):
        lo, hi = hh * X_HEAD_DIM, (hh + 1) * X_HEAD_DIM
        s = _dot_nt(q[:, lo:hi], kv_ref[0, 0, :, lo:hi])
        p = jnp.exp(s - jnp.max(s, axis=-1, keepdims=True))
        l = jnp.sum(p, axis=-1, keepdims=True)
        o = _dot(p.astype(BF16), kv_ref[0, 0, :, D_MODEL + lo:D_MODEL + hi]) / l
        heads.append(o.astype(BF16))
    o_all = jnp.concatenate(heads, axis=-1)
    o_ref[0] = h1 + _dot(o_all, wo_ref[0])


def _mix_xattn(h, attn, conv, w_mo, g, w_xq, kv, w_xo, layer, w_layer, casts):
    b, s, d = h.shape
    tm = ROW_TILE
    n_j = s // tm
    cast_in, cast_out, cast_shapes = _cast_plan(casts, b * n_j, lambda i, j: i * n_j + j)
    in_specs = [
        pl.BlockSpec((1, tm, d), lambda i, j: (i, j, 0)),
        pl.BlockSpec((1, tm, ATTN_WIDTH), lambda i, j: (i, j, 0)),
        pl.BlockSpec((1, tm, CONV_WIDTH_CH), lambda i, j: (i, j, 0)),
        _layer_resident((d, d), w_layer),
        _layer_resident((1, d), layer),
        _layer_resident((d, d), w_layer),
        pl.BlockSpec((1, 1, MEM_LEN, 2 * d), lambda i, j: (layer, i, 0, 0)),
        _layer_resident((d, d), w_layer),
    ]
    return pl.pallas_call(
        _with_casts(_mix_xattn_kernel, len(in_specs), 1, len(casts)),
        grid=(b, n_j),
        in_specs=in_specs + cast_in,
        out_specs=[pl.BlockSpec((1, tm, d), lambda i, j: (i, j, 0))] + cast_out,
        out_shape=[jax.ShapeDtypeStruct((b, s, d), F32)] + cast_shapes,
        compiler_params=pltpu.CompilerParams(
            dimension_semantics=("arbitrary", "arbitrary"), vmem_limit_bytes=VMEM_LIMIT),
        name="mix_xattn",
    )(h, attn, conv, w_mo, g, w_xq, kv, w_xo, *(w for w, _ in casts))


def _ffn_kernel(h_ref, g_ref, w1_ref, w2_ref, fg_ref, o_ref, acc_ref, *, final_norm):
    tm = h_ref.shape[0]
    xn = _rms(h_ref[...], g_ref[0], NORM_EPS).astype(BF16)
    n_chunks = D_FF // FF_CHUNK
    for c in range(n_chunks):
        lo, hi = c * FF_CHUNK, (c + 1) * FF_CHUNK
        f = jnp.square(jnp.maximum(_dot(xn, w1_ref[0, :, lo:hi]), 0.0)).astype(BF16)
        if c == 0:
            acc_ref[...] = _dot(f, w2_ref[0, lo:hi, :])
        elif c < n_chunks - 1:
            acc_ref[...] += _dot(f, w2_ref[0, lo:hi, :])
        else:
            for r in range(0, tm, tm // 2):
                rows = slice(r, r + tm // 2)
                out = h_ref[rows, :] + acc_ref[rows, :] + _dot(f[rows, :], w2_ref[0, lo:hi, :])
                if final_norm:
                    out = _rms(out, fg_ref[...], NORM_EPS)
                o_ref[rows, :] = out


def _ffn(h2d, g, w1, w2, final_g, final_norm, layer, w_layer):
    n, d = h2d.shape
    tm = ROW_TILE
    return pl.pallas_call(
        functools.partial(_ffn_kernel, final_norm=final_norm),
        grid=(n // tm,),
        in_specs=[
            pl.BlockSpec((tm, d), lambda i: (i, 0)),
            _layer_resident((1, d), layer),
            _layer_resident((d, D_FF), w_layer),
            _layer_resident((D_FF, d), w_layer),
            _resident((1, d)),
        ],
        out_specs=pl.BlockSpec((tm, d), lambda i: (i, 0)),
        out_shape=jax.ShapeDtypeStruct((n, d), F32),
        scratch_shapes=[pltpu.VMEM((tm, d), F32)],
        compiler_params=pltpu.CompilerParams(
            dimension_semantics=("arbitrary",), vmem_limit_bytes=VMEM_LIMIT),
        name="ffn",
    )(h2d, g, w1, w2, final_g)


def _mem_kv_kernel(m_ref, g_ref, w_ref, o_ref):
    mn = _rms(m_ref[...], g_ref[...], NORM_EPS).astype(BF16)
    for l in range(w_ref.shape[0]):
        o_ref[l] = _dot(mn, w_ref[l]).astype(BF16)


def _mem_kv(mem2d, g, w_xkv, casts):
    n, d = mem2d.shape
    depth, _, n_cols = w_xkv.shape
    tm = ROW_TILE // 2
    n_i = n // tm
    cast_in, cast_out, cast_shapes = _cast_plan(casts, n_i, lambda i: i)
    in_specs = [
        pl.BlockSpec((tm, d), lambda i: (i, 0)),
        _resident((1, d)),
        _resident((depth, d, n_cols)),
    ]
    return pl.pallas_call(
        _with_casts(_mem_kv_kernel, len(in_specs), 1, len(casts)),
        grid=(n_i,),
        in_specs=in_specs + cast_in,
        out_specs=[pl.BlockSpec((depth, tm, n_cols), lambda i: (0, i, 0))] + cast_out,
        out_shape=[jax.ShapeDtypeStruct((depth, n, n_cols), BF16)] + cast_shapes,
        compiler_params=pltpu.CompilerParams(
            dimension_semantics=("arbitrary",), vmem_limit_bytes=VMEM_LIMIT),
        name="mem_kv",
    )(mem2d, g, w_xkv, *(w for w, _ in casts))


def kernel(x, mem, positions, norm_mix_g, w_in, lam_q1, lam_k1, lam_q2, lam_k2, subln_g, conv_w,
           w_mix_out, norm_x_g, mem_norm_g, w_xq, w_xkv, w_xo, norm_ffn_g, w_ff1, w_ff2, final_g):
    b, s, d = x.shape
    aw = ATTN_WIDTH
    pos3 = positions.reshape(b, 1, s)
    inv_freq = (ROPE_THETA ** (-jnp.arange(0, ROT_DIM, 2, dtype=F32) / ROT_DIM)).reshape(ROT_DIM // 2, 1)
    w_vt_b = jnp.swapaxes(lax.optimization_barrier(w_in[:, :, 2 * aw:3 * aw]), 1, 2).astype(BF16)
    g_mix, g_x, g_ffn = (g.reshape(DEPTH, 1, d) for g in (norm_mix_g, norm_x_g, norm_ffn_g))
    lam_rows = jnp.stack([lam_q1, lam_k1, lam_q2, lam_k2], axis=1).astype(F32)
    subln_col = subln_g.reshape(DEPTH, HEAD_W, 1)

    kv, w_in_b = _mem_kv(mem.reshape(b * MEM_LEN, d), mem_norm_g.reshape(1, d), w_xkv.astype(BF16),
                         casts=[(w_in, 0)])
    kv = kv.reshape(DEPTH, b, MEM_LEN, 2 * d)
    h = x
    for l in range(DEPTH):
        lambda_init = 0.8 - 0.6 * math.exp(-0.3 * l)
        q, k, vt, conv, w_mo_b, w_xq_b, w_xo_b = _in_proj(
            h, pos3, inv_freq, g_mix, w_in_b, 0, w_vt_b, conv_w, l,
            casts=[(w_mix_out, l), (w_xq, l), (w_xo, l)])
        attn = _diff_attn(lam_rows, q, k, vt, subln_col, lambda_init, l)
        next_w_in = [(w_in, l + 1)] if l + 1 < DEPTH else []
        h, w_ff1_b, w_ff2_b, *w_in_next = _mix_xattn(
            h, attn, conv, w_mo_b, g_x, w_xq_b, kv, w_xo_b, l, 0,
            casts=[(w_ff1, l), (w_ff2, l)] + next_w_in)
        if w_in_next:
            w_in_b = w_in_next[0]
        h = _ffn(h.reshape(b * s, d), g_ffn, w_ff1_b, w_ff2_b, final_g.reshape(1, d),
                 l == DEPTH - 1, l, 0).reshape(b, s, d)
    return h
```

```python
import functools
import math

import jax
import jax.numpy as jnp
from jax import lax
from jax.experimental import pallas as pl
from jax.experimental.pallas import tpu as pltpu

D_MODEL = 1024
DEPTH = 2
MEM_LEN = 256
ATTN_WIDTH = D_MODEL // 2
CONV_WIDTH_CH = D_MODEL - ATTN_WIDTH
DIFF_HEAD_DIM = 64
DIFF_HEADS = ATTN_WIDTH // (2 * DIFF_HEAD_DIM)
HEAD_W = 2 * DIFF_HEAD_DIM
CONV_K = 3
ROT_DIM = DIFF_HEAD_DIM // 4
ROPE_THETA = 500000.0
X_HEADS = 4
X_HEAD_DIM = D_MODEL // X_HEADS
D_FF = 4 * D_MODEL
NEG_INF = -1e30
NORM_EPS = 1e-6
SUBLN_EPS = 1e-5

LANES = 128
SUBLANES = 8
VMEM_LIMIT = 56 * 1024 * 1024

ROW_TILE = 1024
Q_TILE = 512
ATTN_HEADS_PER_STEP = 2
SUM_ROWS = 16
FF_CHUNK = 1024

BF16 = jnp.bfloat16
F32 = jnp.float32


def _rms(x, g, eps):
    return x * lax.rsqrt(jnp.mean(x * x, axis=-1, keepdims=True) + eps) * g


def _dot(a, b):
    return jnp.dot(a, b, preferred_element_type=F32)


def _dot_nt(a, b):
    return lax.dot_general(a, b, (((1,), (1,)), ((), ())), preferred_element_type=F32)


def _resident(shape):
    zeros = (0,) * len(shape)
    return pl.BlockSpec(shape, lambda *_: zeros, pipeline_mode=pl.Buffered(1))


def _layer_resident(shape, layer):
    index = (layer,) + (0,) * len(shape)
    return pl.BlockSpec((1,) + tuple(shape), lambda *_: index, pipeline_mode=pl.Buffered(1))


def _cast_plan(casts, n_steps, step_of):
    in_specs, out_specs, out_shapes = [], [], []
    for w, layer in casts:
        r, c = w.shape[1:]
        rows = r // n_steps
        in_specs.append(pl.BlockSpec((1, rows, c), lambda *g, layer=layer: (layer, step_of(*g), 0)))
        out_specs.append(pl.BlockSpec((1, rows, c), lambda *g: (0, step_of(*g), 0)))
        out_shapes.append(jax.ShapeDtypeStruct((1, r, c), BF16))
    return in_specs, out_specs, out_shapes


def _with_casts(body, n_in, n_out, n_cast):
    def kernel(*refs):
        ins, cast_in = refs[:n_in], refs[n_in:n_in + n_cast]
        outs = refs[n_in + n_cast:n_in + n_cast + n_out]
        cast_out = refs[n_in + n_cast + n_out:n_in + 2 * n_cast + n_out]
        scratch = refs[n_in + 2 * n_cast + n_out:]
        for src, dst in zip(cast_in, cast_out):
            dst[...] = src[...].astype(BF16)
        body(*ins, *outs, *scratch)
    return kernel


def _in_proj_kernel(pos_ref, inv_freq_ref, x_ref, g_ref, w_ref, wvt_ref, cw_ref,
                    q_ref, k_ref, vt_ref, c_ref, carry_ref):
    tm = x_ref.shape[1]
    aw = ATTN_WIDTH

    @pl.when(pl.program_id(1) == 0)
    def _():
        carry_ref[...] = jnp.zeros_like(carry_ref)

    xn = _rms(x_ref[0], g_ref[0], NORM_EPS).astype(BF16)

    half = ROT_DIM // 2
    ang_t = inv_freq_ref[...] * pos_ref[0].astype(F32)
    cos_t, sin_t = jnp.cos(ang_t), jnp.sin(ang_t)
    rest = DIFF_HEAD_DIM - ROT_DIM
    ones, zeros = jnp.ones((rest, tm), F32), jnp.zeros((rest, tm), F32)
    cos = jnp.concatenate([cos_t, cos_t, ones] * 2, axis=0).T
    sin = jnp.concatenate([-sin_t, sin_t, zeros] * 2, axis=0).T
    lane = lax.broadcasted_iota(jnp.int32, (tm, HEAD_W), 1)
    pairs_up = (lane & (DIFF_HEAD_DIM - 1)) < half

    def rotate(t):
        up = pltpu.roll(t, LANES - half, axis=1)
        dn = pltpu.roll(t, half, axis=1)
        return t * cos + jnp.where(pairs_up, up, dn) * sin

    scale = math.log2(math.e) * DIFF_HEAD_DIM ** -0.5
    q = _dot(xn, w_ref[0, :, 0:aw])
    k = _dot(xn, w_ref[0, :, aw:2 * aw])
    for h in range(DIFF_HEADS):
        lo, hi = h * HEAD_W, (h + 1) * HEAD_W
        q_ref[0, :, lo:hi] = (rotate(q[:, lo:hi]) * scale).astype(BF16)
        k_ref[0, :, lo:hi] = rotate(k[:, lo:hi]).astype(BF16)

    cwid = CONV_WIDTH_CH
    c_gate = _dot(xn, w_ref[0, :, 3 * aw + cwid:3 * aw + 2 * cwid])
    hc = _dot(xn, w_ref[0, :, 3 * aw + 2 * cwid:3 * aw + 3 * cwid])
    b_gate = _dot(xn, w_ref[0, :, 3 * aw:3 * aw + cwid])
    u = c_gate * hc
    prev = carry_ref[...]
    row = lax.broadcasted_iota(jnp.int32, u.shape, 0)
    last = prev[SUBLANES - 1:SUBLANES, :]
    last2 = prev[SUBLANES - 2:SUBLANES - 1, :]
    u1 = jnp.where(row == 0, last, pltpu.roll(u, 1, axis=0))
    u2 = jnp.where(row == 0, last2, jnp.where(row == 1, last, pltpu.roll(u, 2, axis=0)))
    cw = cw_ref[0]
    y = cw[0:1, :] * u2 + cw[1:2, :] * u1 + cw[2:3, :] * u
    c_ref[0] = (b_gate * y).astype(BF16)
    carry_ref[...] = u[tm - SUBLANES:tm, :]

    vt = _dot_nt(wvt_ref[0], xn).astype(BF16)
    tk = vt_ref.shape[3]
    for i in range(tm // tk):
        vt_ref[0, i] = vt[:, i * tk:(i + 1) * tk]


def _in_proj(h, pos3, inv_freq, g, w_in, w_in_layer, w_vt, conv_w, layer, casts):
    b, s, d = h.shape
    tm = ROW_TILE
    n_cols = w_in.shape[2]
    act = jax.ShapeDtypeStruct((b, s, ATTN_WIDTH), BF16)
    act_spec = pl.BlockSpec((1, tm, ATTN_WIDTH), lambda i, j: (i, j, 0))
    tk = Q_TILE
    vt = jax.ShapeDtypeStruct((b, s // tk, ATTN_WIDTH, tk), BF16)
    vt_spec = pl.BlockSpec((1, tm // tk, ATTN_WIDTH, tk), lambda i, j: (i, j, 0, 0))
    n_j = s // tm
    cast_in, cast_out, cast_shapes = _cast_plan(casts, b * n_j, lambda i, j: i * n_j + j)
    in_specs = [
        pl.BlockSpec((1, 1, tm), lambda i, j: (i, 0, j)),
        _resident((ROT_DIM // 2, 1)),
        pl.BlockSpec((1, tm, d), lambda i, j: (i, j, 0)),
        _layer_resident((1, d), layer),
        _layer_resident((d, n_cols), w_in_layer),
        _layer_resident((ATTN_WIDTH, d), layer),
        _layer_resident((CONV_K, CONV_WIDTH_CH), layer),
    ]
    return pl.pallas_call(
        _with_casts(_in_proj_kernel, len(in_specs), 4, len(casts)),
        grid=(b, n_j),
        in_specs=in_specs + cast_in,
        out_specs=[act_spec, act_spec, vt_spec, act_spec] + cast_out,
        out_shape=[act, act, vt, act] + cast_shapes,
        scratch_shapes=[pltpu.VMEM((SUBLANES, CONV_WIDTH_CH), F32)],
        compiler_params=pltpu.CompilerParams(
            dimension_semantics=("arbitrary", "arbitrary"), vmem_limit_bytes=VMEM_LIMIT),
        name="in_proj",
    )(pos3, inv_freq, h, g, w_in, w_vt, conv_w, *(w for w, _ in casts))


def _diff_attn_kernel(lam_ref, q_ref, k_ref, vt_ref, g_ref, o_ref, *, out_scale, lambda_init):
    tq = Q_TILE
    tk = vt_ref.shape[3]
    n_q = q_ref.shape[1] // tq
    heads = range(q_ref.shape[2] // HEAD_W)
    lam_rows = lam_ref[0]
    lam = (jnp.exp(jnp.sum(lam_rows[0:1, :] * lam_rows[1:2, :], axis=-1, keepdims=True))
           - jnp.exp(jnp.sum(lam_rows[2:3, :] * lam_rows[3:4, :], axis=-1, keepdims=True))
           + lambda_init)
    g_col = g_ref[0]
    ones_rows = jnp.ones((SUM_ROWS, tk), BF16)
    hq = tq // 2

    def pieces(qi, j):
        if (j + 1) * tk >= (qi + 1) * tq:
            return [(0, hq, hq, True), (hq, tq - hq, tk, True)]
        return [(0, tq, tk, False)]

    def scores(hd, qi, j):
        lanes = slice(hd * HEAD_W, (hd + 1) * HEAD_W)
        out = []
        for c0, nc, nk, _ in pieces(qi, j):
            qh = q_ref[0, qi * tq + c0:qi * tq + c0 + nc, lanes]
            first = lax.broadcasted_iota(jnp.int32, qh.shape, 1) < DIFF_HEAD_DIM
            zero = jnp.zeros_like(qh)
            kj = k_ref[0, j * tk:j * tk + nk, lanes]
            out.append((_dot_nt(kj, jnp.where(first, qh, zero)), _dot_nt(kj, jnp.where(first, zero, qh))))
        return out

    def finalize(hd, row0, state):
        (_, l0, a0), (_, l1, a1) = state
        ot = a0 / l0 - lam * (a1 / l1)
        ms = jnp.mean(ot * ot, axis=0, keepdims=True)
        ot = ot * lax.rsqrt(ms + SUBLN_EPS) * g_col * out_scale
        o_ref[0, row0:row0 + ot.shape[1], hd * HEAD_W:(hd + 1) * HEAD_W] = ot.T.astype(o_ref.dtype)

    def softmax_pv(hd, qi, j, s_cur, state):
        vt_aug = jnp.concatenate([vt_ref[0, j, hd * HEAD_W:(hd + 1) * HEAD_W, :], ones_rows], axis=0)
        new_state = []
        for (c0, nc, nk, masked), s_piece in zip(pieces(qi, j), s_cur):
            piece_state = []
            for c in range(2):
                s = s_piece[c]
                if masked:
                    key = lax.broadcasted_iota(jnp.int32, (nk, nc), 0)
                    qry = lax.broadcasted_iota(jnp.int32, (nk, nc), 1) + c0
                    s = jnp.where(key <= qry, s, NEG_INF)
                m_new = jnp.max(s, axis=0, keepdims=True)
                if j > 0:
                    m_old, l_old, acc_old = (a[:, c0:c0 + nc] for a in state[c])
                    m_new = jnp.maximum(m_old, m_new)
                    alpha = jnp.exp2(m_old - m_new)
                pv = _dot(vt_aug[:, 0:nk], jnp.exp2(s - m_new).astype(BF16))
                acc, l = pv[:HEAD_W], pv[HEAD_W:HEAD_W + 1]
                if j > 0:
                    acc, l = alpha * acc_old + acc, alpha * l_old + l
                piece_state.append((m_new, l, acc))
            if masked:
                finalize(hd, qi * tq + c0, piece_state)
            else:
                new_state = piece_state
        return new_state

    pairs = [(qi, j) for qi in range(n_q) for j in range((qi * tq) // tk + 1)]
    s_next = [scores(hd, *pairs[0]) for hd in heads]
    state = [None for _ in heads]
    for t, (qi, j) in enumerate(pairs):
        s_cur = s_next
        if t + 1 < len(pairs):
            s_next = [scores(hd, *pairs[t + 1]) for hd in heads]
        state = [softmax_pv(hd, qi, j, s_cur[hd], state[hd]) for hd in heads]


def _diff_attn(lam_rows, q, k, vt, subln_g_col, lambda_init, layer):
    b, s, w = q.shape
    nk, tk = vt.shape[1], vt.shape[3]
    assert Q_TILE == tk, "diagonal tiles are assumed square"
    gw = ATTN_HEADS_PER_STEP * HEAD_W
    kern = functools.partial(_diff_attn_kernel, out_scale=1.0 - lambda_init, lambda_init=lambda_init)
    return pl.pallas_call(
        kern,
        grid=(b, DIFF_HEADS // ATTN_HEADS_PER_STEP),
        in_specs=[
            _layer_resident((4, DIFF_HEAD_DIM), layer),
            pl.BlockSpec((1, s, gw), lambda i, h: (i, 0, h)),
            pl.BlockSpec((1, s, gw), lambda i, h: (i, 0, h)),
            pl.BlockSpec((1, nk, gw, tk), lambda i, h: (i, 0, h, 0)),
            _layer_resident((HEAD_W, 1), layer),
        ],
        out_specs=pl.BlockSpec((1, s, gw), lambda i, h: (i, 0, h)),
        out_shape=jax.ShapeDtypeStruct((b, s, w), BF16),
        compiler_params=pltpu.CompilerParams(
            dimension_semantics=("arbitrary", "arbitrary"), vmem_limit_bytes=VMEM_LIMIT),
        name="diff_attn",
    )(lam_rows, q, k, vt, subln_g_col)


def _mix_xattn_kernel(h_ref, a_ref, c_ref, wmo_ref, g_ref, wq_ref, kv_ref, wo_ref, o_ref):
    aw = ATTN_WIDTH
    scale = math.log2(math.e) * X_HEAD_DIM ** -0.5
    h1 = h_ref[0] + _dot(a_ref[0], wmo_ref[0, 0:aw, :]) + _dot(c_ref[0], wmo_ref[0, aw:, :])
    xn = _rms(h1, g_ref[0], NORM_EPS).astype(BF16)
    q = (_dot(xn, wq_ref[0]) * scale).astype(BF16)
    heads = []
    for hh in range(X_HEADS):
        lo, hi = hh * X_HEAD_DIM, (hh + 1) * X_HEAD_DIM
        s = _dot_nt(q[:, lo:hi], kv_ref[0, 0, :, lo:hi])
        p = jnp.exp2(s - jnp.max(s, axis=-1, keepdims=True))
        l = jnp.sum(p, axis=-1, keepdims=True)
        o = _dot(p.astype(BF16), kv_ref[0, 0, :, D_MODEL + lo:D_MODEL + hi]) / l
        heads.append(o.astype(BF16))
    o_all = jnp.concatenate(heads, axis=-1)
    o_ref[0] = h1 + _dot(o_all, wo_ref[0])


def _mix_xattn(h, attn, conv, w_mo, g, w_xq, kv, w_xo, layer, w_layer, casts):
    b, s, d = h.shape
    tm = ROW_TILE
    n_j = s // tm
    cast_in, cast_out, cast_shapes = _cast_plan(casts, b * n_j, lambda i, j: i * n_j + j)
    in_specs = [
        pl.BlockSpec((1, tm, d), lambda i, j: (i, j, 0)),
        pl.BlockSpec((1, tm, ATTN_WIDTH), lambda i, j: (i, j, 0)),
        pl.BlockSpec((1, tm, CONV_WIDTH_CH), lambda i, j: (i, j, 0)),
        _layer_resident((d, d), w_layer),
        _layer_resident((1, d), layer),
        _layer_resident((d, d), w_layer),
        pl.BlockSpec((1, 1, MEM_LEN, 2 * d), lambda i, j: (layer, i, 0, 0)),
        _layer_resident((d, d), w_layer),
    ]
    return pl.pallas_call(
        _with_casts(_mix_xattn_kernel, len(in_specs), 1, len(casts)),
        grid=(b, n_j),
        in_specs=in_specs + cast_in,
        out_specs=[pl.BlockSpec((1, tm, d), lambda i, j: (i, j, 0))] + cast_out,
        out_shape=[jax.ShapeDtypeStruct((b, s, d), F32)] + cast_shapes,
        compiler_params=pltpu.CompilerParams(
            dimension_semantics=("arbitrary", "arbitrary"), vmem_limit_bytes=VMEM_LIMIT),
        name="mix_xattn",
    )(h, attn, conv, w_mo, g, w_xq, kv, w_xo, *(w for w, _ in casts))


def _ffn_kernel(h_ref, g_ref, w1_ref, w2_ref, fg_ref, o_ref, acc_ref, *, final_norm):
    h = h_ref[...]
    xn = _rms(h, g_ref[0], NORM_EPS).astype(BF16)
    for c in range(D_FF // FF_CHUNK):
        lo, hi = c * FF_CHUNK, (c + 1) * FF_CHUNK
        f = jnp.square(jnp.maximum(_dot(xn, w1_ref[0, :, lo:hi]), 0.0)).astype(BF16)
        part = _dot(f, w2_ref[0, lo:hi, :])
        if c == 0:
            acc_ref[...] = part
        else:
            acc_ref[...] += part
    out = h + acc_ref[...]
    if final_norm:
        out = _rms(out, fg_ref[...], NORM_EPS)
    o_ref[...] = out


def _ffn(h2d, g, w1, w2, final_g, final_norm, layer, w_layer):
    n, d = h2d.shape
    tm = ROW_TILE
    return pl.pallas_call(
        functools.partial(_ffn_kernel, final_norm=final_norm),
        grid=(n // tm,),
        in_specs=[
            pl.BlockSpec((tm, d), lambda i: (i, 0)),
            _layer_resident((1, d), layer),
            _layer_resident((d, D_FF), w_layer),
            _layer_resident((D_FF, d), w_layer),
            _resident((1, d)),
        ],
        out_specs=pl.BlockSpec((tm, d), lambda i: (i, 0)),
        out_shape=jax.ShapeDtypeStruct((n, d), F32),
        scratch_shapes=[pltpu.VMEM((tm, d), F32)],
        compiler_params=pltpu.CompilerParams(
            dimension_semantics=("arbitrary",), vmem_limit_bytes=VMEM_LIMIT),
        name="ffn",
    )(h2d, g, w1, w2, final_g)


def _mem_kv_kernel(m_ref, g_ref, w_ref, o_ref):
    mn = _rms(m_ref[...], g_ref[...], NORM_EPS).astype(BF16)
    o_ref[0] = _dot(mn, w_ref[0]).astype(BF16)


def _mem_kv(mem2d, g, w_xkv, casts):
    n, d = mem2d.shape
    depth, _, n_cols = w_xkv.shape
    tm = ROW_TILE
    n_i = n // tm
    cast_in, cast_out, cast_shapes = _cast_plan(casts, depth * n_i, lambda l, i: l * n_i + i)
    in_specs = [
        pl.BlockSpec((tm, d), lambda l, i: (i, 0)),
        pl.BlockSpec((1, d), lambda l, i: (0, 0)),
        pl.BlockSpec((1, d, n_cols), lambda l, i: (l, 0, 0)),
    ]
    return pl.pallas_call(
        _with_casts(_mem_kv_kernel, len(in_specs), 1, len(casts)),
        grid=(depth, n_i),
        in_specs=in_specs + cast_in,
        out_specs=[pl.BlockSpec((1, tm, n_cols), lambda l, i: (l, i, 0))] + cast_out,
        out_shape=[jax.ShapeDtypeStruct((depth, n, n_cols), BF16)] + cast_shapes,
        compiler_params=pltpu.CompilerParams(
            dimension_semantics=("arbitrary", "arbitrary"), vmem_limit_bytes=VMEM_LIMIT),
        name="mem_kv",
    )(mem2d, g, w_xkv, *(w for w, _ in casts))


def kernel(x, mem, positions, norm_mix_g, w_in, lam_q1, lam_k1, lam_q2, lam_k2, subln_g, conv_w,
           w_mix_out, norm_x_g, mem_norm_g, w_xq, w_xkv, w_xo, norm_ffn_g, w_ff1, w_ff2, final_g):
    b, s, d = x.shape
    aw = ATTN_WIDTH
    pos3 = positions.reshape(b, 1, s)
    inv_freq = (ROPE_THETA ** (-jnp.arange(0, ROT_DIM, 2, dtype=F32) / ROT_DIM)).reshape(ROT_DIM // 2, 1)
    w_vt_b = jnp.swapaxes(lax.optimization_barrier(w_in[:, :, 2 * aw:3 * aw]), 1, 2).astype(BF16)
    g_mix, g_x, g_ffn = (g.reshape(DEPTH, 1, d) for g in (norm_mix_g, norm_x_g, norm_ffn_g))
    lam_rows = jnp.stack([lam_q1, lam_k1, lam_q2, lam_k2], axis=1).astype(F32)
    subln_col = subln_g.reshape(DEPTH, HEAD_W, 1)

    kv, w_in_b = _mem_kv(mem.reshape(b * MEM_LEN, d), mem_norm_g.reshape(1, d), w_xkv.astype(BF16),
                         casts=[(w_in, 0)])
    kv = kv.reshape(DEPTH, b, MEM_LEN, 2 * d)
    h = x
    for l in range(DEPTH):
        lambda_init = 0.8 - 0.6 * math.exp(-0.3 * l)
        q, k, vt, conv, w_mo_b, w_xq_b, w_xo_b = _in_proj(
            h, pos3, inv_freq, g_mix, w_in_b, 0, w_vt_b, conv_w, l,
            casts=[(w_mix_out, l), (w_xq, l), (w_xo, l)])
        attn = _diff_attn(lam_rows, q, k, vt, subln_col, lambda_init, l)
        next_w_in = [(w_in, l + 1)] if l + 1 < DEPTH else []
        h, w_ff1_b, w_ff2_b, *w_in_next = _mix_xattn(
            h, attn, conv, w_mo_b, g_x, w_xq_b, kv, w_xo_b, l, 0,
            casts=[(w_ff1, l), (w_ff2, l)] + next_w_in)
        if w_in_next:
            w_in_b = w_in_next[0]
        h = _ffn(h.reshape(b * s, d), g_ffn, w_ff1_b, w_ff2_b, final_g.reshape(1, d),
                 l == DEPTH - 1, l, 0).reshape(b, s, d)
    return h
```
